```python
import math
import jax, jax.numpy as jnp
from jax import lax
import numpy as np

D_MODEL = 2048
BATCH = 8
SEQ = 2048
DEPTH = 2

DIFF_HEAD_DIM = 128
ATTN_WIDTH = D_MODEL // 2
DIFF_HEADS = ATTN_WIDTH // (2 * DIFF_HEAD_DIM)
QK_WIDTH = DIFF_HEADS * 2 * DIFF_HEAD_DIM
POOL_WINDOWS = (2, 4, 8, 16)
POOL_WIDTH = D_MODEL // 4
POOL_GROUP = POOL_WIDTH // len(POOL_WINDOWS)
CONV_WIDTH = D_MODEL // 4
CONV_TAPS = 3
IN_WIDTH = 2 * QK_WIDTH + ATTN_WIDTH + POOL_WIDTH + 3 * CONV_WIDTH
SPLIT_POINTS = (QK_WIDTH, 2 * QK_WIDTH, 2 * QK_WIDTH + ATTN_WIDTH,
                2 * QK_WIDTH + ATTN_WIDTH + POOL_WIDTH,
                2 * QK_WIDTH + ATTN_WIDTH + POOL_WIDTH + CONV_WIDTH,
                2 * QK_WIDTH + ATTN_WIDTH + POOL_WIDTH + 2 * CONV_WIDTH)
ROPE_THETA = 10000.0
Q_BLOCK = 128
D_FF_DENSE = 5632
N_EXPERTS = 8
TOP_K = 2
D_FF_EXPERT = 7168
N_DENSE = (DEPTH + 1) // 2
N_MOE = DEPTH // 2
EPS = 1e-6
NEG_INF = -1e30

kernel_name = 'hybrid_diffattn_pool_conv_moe'


def rms_norm(x, g):
    xf = x.astype(jnp.float32)
    y = xf * lax.rsqrt(jnp.mean(xf * xf, axis=-1, keepdims=True) + EPS)
    return y.astype(x.dtype) * g


def rotary(x, cos, sin):
    x1, x2 = jnp.split(x, 2, axis=-1)
    return jnp.concatenate([x1 * cos - x2 * sin, x1 * sin + x2 * cos], axis=-1)


def diff_attention(q, k, v, lam):
    B, S, H, _, d = q.shape
    nb = S // Q_BLOCK
    scale = d ** -0.5
    qb = q.reshape(B, nb, Q_BLOCK, H, 2, d).transpose(1, 0, 2, 3, 4, 5)
    key_pos = jnp.arange(S)

    def block(args):
        i, qi = args
        s = jnp.einsum('bqhmd,bkhmd->bhmqk', qi, k).astype(jnp.float32) * scale
        q_pos = i * Q_BLOCK + jnp.arange(Q_BLOCK)
        mask = key_pos[None, :] <= q_pos[:, None]
        p = jax.nn.softmax(jnp.where(mask, s, NEG_INF), axis=-1)
        a = p[:, :, 0] - lam * p[:, :, 1]
        return jnp.einsum('bhqk,bkhe->bqhe', a.astype(v.dtype), v)

    o = lax.map(block, (jnp.arange(nb), qb))
    return o.transpose(1, 0, 2, 3, 4).reshape(B, S, H, 2 * d)


def pool_mixer(u, w_pool, pool_scale):
    B, S, _ = u.shape
    uf = u.astype(jnp.float32).reshape(B, S, len(POOL_WINDOWS), POOL_GROUP)
    outs = []
    for gi, w in enumerate(POOL_WINDOWS):
        g = uf[:, :, gi]
        c = jnp.cumsum(g, axis=1)
        c_shift = jnp.pad(c, ((0, 0), (w, 0), (0, 0)))[:, :S]
        count = jnp.minimum(jnp.arange(S) + 1, w).astype(jnp.float32)
        outs.append((c - c_shift) / count[None, :, None] - g)
    pooled = jnp.stack(outs, axis=2).astype(u.dtype)
    y = jnp.einsum('bsgc,gcd->bsgd', pooled, w_pool)
    return y.reshape(B, S, POOL_WIDTH) * pool_scale


def short_conv_mixer(gate_b, gate_c, h, conv_w):
    S = h.shape[1]
    u = jnp.pad(gate_c * h, ((0, 0), (CONV_TAPS - 1, 0), (0, 0)))
    y = sum(conv_w[j] * u[:, j:j + S] for j in range(CONV_TAPS))
    return gate_b * y


def hybrid_mixer(h, cos, sin, lam_init, w_in, q_g, k_g, lam_vecs, sub_g, w_pool, pool_scale, conv_w, w_out):
    B, S, _ = h.shape
    z = h @ w_in
    q, k, v, u_pool, gate_b, gate_c, u_conv = jnp.split(z, SPLIT_POINTS, axis=-1)
    q = rotary(rms_norm(q.reshape(B, S, DIFF_HEADS, 2, DIFF_HEAD_DIM), q_g), cos, sin)
    k = rotary(rms_norm(k.reshape(B, S, DIFF_HEADS, 2, DIFF_HEAD_DIM), k_g), cos, sin)
    v = v.reshape(B, S, DIFF_HEADS, 2 * DIFF_HEAD_DIM)
    lf = lam_vecs.astype(jnp.float32)
    lam = jnp.exp(jnp.sum(lf[0] * lf[1])) - jnp.exp(jnp.sum(lf[2] * lf[3])) + lam_init
    attn = rms_norm(diff_attention(q, k, v, lam), sub_g) * (1.0 - lam_init)
    attn = attn.reshape(B, S, ATTN_WIDTH)
    pool = pool_mixer(u_pool, w_pool, pool_scale)
    conv = short_conv_mixer(gate_b, gate_c, u_conv, conv_w)
    return jnp.concatenate([attn, pool, conv], axis=-1) @ w_out


def swiglu(h, wg, wu, wd):
    return (jax.nn.silu(h @ wg) * (h @ wu)) @ wd


def moe_swiglu(h, router_w, wg, wu, wd):
    B, S, D = h.shape
    t = h.reshape(B * S, D)
    logits = (t @ router_w).astype(jnp.float32)
    top_vals, top_idx = lax.top_k(logits, TOP_K)
    gates = jax.nn.softmax(top_vals, axis=-1)
    combine = jnp.sum(jax.nn.one_hot(top_idx, N_EXPERTS, dtype=jnp.float32) * gates[..., None], axis=1)
    combine = combine.astype(t.dtype)
    out = jnp.zeros_like(t)
    for e in range(N_EXPERTS):
        out = out + combine[:, e:e + 1] * swiglu(t, wg[e], wu[e], wd[e])
    return out.reshape(B, S, D)


def setup_inputs(seed: int = 0) -> dict:
    key = jax.random.key(seed)
    ks = jax.random.split(key, 20)
    f32 = jnp.float32
    nrm = lambda k, shape, s: jax.random.normal(k, shape, f32) * s
    d = DIFF_HEAD_DIM
    offsets = jax.random.randint(ks[1], (BATCH, 1), 0, 4096, dtype=jnp.int32)
    return {
        'x': nrm(ks[0], (BATCH, SEQ, D_MODEL), 1.0),
        'positions': offsets + jnp.arange(SEQ, dtype=jnp.int32)[None, :],
        'attn_norm_g': 1.0 + nrm(ks[2], (DEPTH, D_MODEL), 0.02),
        'w_in': nrm(ks[3], (DEPTH, D_MODEL, IN_WIDTH), D_MODEL ** -0.5),
        'q_norm_g': 1.0 + nrm(ks[4], (DEPTH, d), 0.02),
        'k_norm_g': 1.0 + nrm(ks[5], (DEPTH, d), 0.02),
        'lambda_vecs': nrm(ks[6], (DEPTH, 4, d), 0.1),
        'attn_out_norm_g': 1.0 + nrm(ks[7], (DEPTH, 2 * d), 0.02),
        'w_pool': nrm(ks[8], (DEPTH, len(POOL_WINDOWS), POOL_GROUP, POOL_GROUP), POOL_GROUP ** -0.5),
        'pool_scale': 1.0 + nrm(ks[9], (DEPTH, POOL_WIDTH), 0.1),
        'conv_w': nrm(ks[10], (DEPTH, CONV_TAPS, CONV_WIDTH), CONV_TAPS ** -0.5),
        'w_out': nrm(ks[11], (DEPTH, D_MODEL, D_MODEL), D_MODEL ** -0.5),
        'ffn_norm_g': 1.0 + nrm(ks[12], (DEPTH, D_MODEL), 0.02),
        'dense_w_gate': nrm(ks[13], (N_DENSE, D_MODEL, D_FF_DENSE), D_MODEL ** -0.5),
        'dense_w_up': nrm(ks[14], (N_DENSE, D_MODEL, D_FF_DENSE), D_MODEL ** -0.5),
        'dense_w_down': nrm(ks[15], (N_DENSE, D_FF_DENSE, D_MODEL), D_FF_DENSE ** -0.5),
        'router_w': nrm(ks[16], (N_MOE, D_MODEL, N_EXPERTS), D_MODEL ** -0.5),
        'moe_w_gate': nrm(ks[17], (N_MOE, N_EXPERTS, D_MODEL, D_FF_EXPERT), D_MODEL ** -0.5),
        'moe_w_up': nrm(ks[18], (N_MOE, N_EXPERTS, D_MODEL, D_FF_EXPERT), D_MODEL ** -0.5),
        'moe_w_down': nrm(ks[19], (N_MOE, N_EXPERTS, D_FF_EXPERT, D_MODEL), D_FF_EXPERT ** -0.5),
    }


def reference(x, positions, attn_norm_g, w_in, q_norm_g, k_norm_g, lambda_vecs, attn_out_norm_g,
              w_pool, pool_scale, conv_w, w_out, ffn_norm_g, dense_w_gate, dense_w_up, dense_w_down,
              router_w, moe_w_gate, moe_w_up, moe_w_down):
    inv_freq = 1.0 / (ROPE_THETA ** (jnp.arange(0, DIFF_HEAD_DIM, 2, dtype=jnp.float32) / DIFF_HEAD_DIM))
    ang = positions.astype(jnp.float32)[..., None] * inv_freq
    cos = jnp.cos(ang).astype(x.dtype)[:, :, None, None, :]
    sin = jnp.sin(ang).astype(x.dtype)[:, :, None, None, :]
    for l in range(DEPTH):
        lam_init = 0.8 - 0.6 * math.exp(-0.3 * l)
        h = rms_norm(x, attn_norm_g[l])
        x = x + hybrid_mixer(h, cos, sin, lam_init, w_in[l], q_norm_g[l], k_norm_g[l], lambda_vecs[l],
                             attn_out_norm_g[l], w_pool[l], pool_scale[l], conv_w[l], w_out[l])
        h = rms_norm(x, ffn_norm_g[l])
        if l % 2 == 0:
            j = l // 2
            x = x + swiglu(h, dense_w_gate[j], dense_w_up[j], dense_w_down[j])
        else:
            j = l // 2
            x = x + moe_swiglu(h, router_w[j], moe_w_gate[j], moe_w_up[j], moe_w_down[j])
    return x
```

```python
import functools
import math

import jax
import jax.numpy as jnp
from jax import lax
from jax.experimental import pallas as pl
from jax.experimental.pallas import tpu as pltpu

D_MODEL = 2048
DIFF_HEAD_DIM = 128
DIFF_HEADS = 4
HEAD_WIDTH = 2 * DIFF_HEAD_DIM
QK_WIDTH = DIFF_HEADS * HEAD_WIDTH
ATTN_WIDTH = QK_WIDTH
POOL_WINDOWS = (2, 4, 8, 16)
POOL_GROUP = 128
POOL_WIDTH = 512
CONV_WIDTH = 512
CONV_TAPS = 3
IN_WIDTH = 5120
ROPE_THETA = 10000.0
N_EXPERTS = 8
TOP_K = 2
EPS = 1e-6
NEG_INF = -1e30
LANES = 128

BF16 = jnp.bfloat16
F32 = jnp.float32

VMEM_LIMIT_BYTES = 56 * 1024 * 1024

IN_TM = 1024
IN_TN = 1024
ATTN_TQ = 512
ATTN_TK = 512
OUT_TM = 512
FFN_TM = 1024
FFN_TF = 256


def _params(semantics):
    return pltpu.CompilerParams(dimension_semantics=semantics, vmem_limit_bytes=VMEM_LIMIT_BYTES)


def _in_proj_kernel(x_ref, g_ref, w_ref, cos_ref, sin_ref, qkg_ref, z_ref, h_ref):
    j = pl.program_id(1)

    @pl.when(j == 0)
    def _():
        x = x_ref[...]
        ms = jnp.mean(x * x, axis=-1, keepdims=True)
        h_ref[...] = (x * lax.rsqrt(ms + EPS) * g_ref[...]).astype(BF16)

    acc = jnp.dot(h_ref[...], w_ref[...], preferred_element_type=F32)

    @pl.when(j < 2)
    def _():
        gain = qkg_ref[j]
        scale = jnp.where(j == 0, DIFF_HEAD_DIM ** -0.5, 1.0).astype(F32)
        cos = cos_ref[...]
        sin = sin_ref[...]
        for c in range(IN_TN // DIFF_HEAD_DIM):
            zc = acc[:, c * DIFF_HEAD_DIM:(c + 1) * DIFF_HEAD_DIM]
            ms = jnp.mean(zc * zc, axis=-1, keepdims=True)
            zn = zc * lax.rsqrt(ms + EPS) * gain
            rot = zn * cos + pltpu.roll(zn, DIFF_HEAD_DIM // 2, axis=1) * sin
            z_ref[:, c * DIFF_HEAD_DIM:(c + 1) * DIFF_HEAD_DIM] = (rot * scale).astype(BF16)

    @pl.when(j >= 2)
    def _():
        z_ref[...] = acc.astype(BF16)


def _in_proj(x, norm_g, w_in, cos2, sin2, qk_g):
    n = x.shape[0]
    grid = (n // IN_TM, IN_WIDTH // IN_TN)
    return pl.pallas_call(
        _in_proj_kernel,
        grid=grid,
        in_specs=[
            pl.BlockSpec((IN_TM, D_MODEL), lambda i, j: (i, 0)),
            pl.BlockSpec((1, D_MODEL), lambda i, j: (0, 0)),
            pl.BlockSpec((D_MODEL, IN_TN), lambda i, j: (0, j)),
            pl.BlockSpec((IN_TM, DIFF_HEAD_DIM), lambda i, j: (i, 0)),
            pl.BlockSpec((IN_TM, DIFF_HEAD_DIM), lambda i, j: (i, 0)),
            pl.BlockSpec((2, 1, DIFF_HEAD_DIM), lambda i, j: (0, 0, 0)),
        ],
        out_specs=pl.BlockSpec((IN_TM, IN_TN), lambda i, j: (i, j)),
        out_shape=jax.ShapeDtypeStruct((n, IN_WIDTH), BF16),
        scratch_shapes=[pltpu.VMEM((IN_TM, D_MODEL), BF16)],
        compiler_params=_params(("parallel", "arbitrary")),
        name="in_proj",
    )(x, norm_g, w_in, cos2, sin2, qk_g)


def _attn_kernel(q_ref, k_ref, v_ref, lv_ref, subg_ref, o_ref, m_ref, l_ref, acc_ref, *, lam_init):
    qi = pl.program_id(2)
    tq, tk, d = ATTN_TQ, ATTN_TK, DIFF_HEAD_DIM
    q = q_ref[...]
    q_maps = (q[:, :d], q[:, d:])

    m_ref[...] = jnp.full(m_ref.shape, NEG_INF, F32)
    l_ref[...] = jnp.zeros(l_ref.shape, F32)
    acc_ref[...] = jnp.zeros(acc_ref.shape, F32)

    def kv_step(j, masked):
        start = pl.multiple_of(j * tk, tk)
        kb = k_ref[pl.ds(start, tk), :]
        vb = v_ref[pl.ds(start, tk), :]
        for mi in range(2):
            km = kb[:, mi * d:(mi + 1) * d]
            s = lax.dot_general(q_maps[mi], km, (((1,), (1,)), ((), ())), preferred_element_type=F32)
            if masked:
                row = lax.broadcasted_iota(jnp.int32, (tq, tk), 0)
                col = lax.broadcasted_iota(jnp.int32, (tq, tk), 1)
                s = jnp.where(col <= row, s, NEG_INF)
            m_old = m_ref[mi]
            m_new = jnp.maximum(m_old, jnp.max(s, axis=-1, keepdims=True))
            alpha = jnp.exp(m_old - m_new)
            p = jnp.exp(s - m_new)
            l_ref[mi] = alpha * l_ref[mi] + jnp.sum(p, axis=-1, keepdims=True)
            acc_ref[mi] = alpha * acc_ref[mi] + jnp.dot(p.astype(BF16), vb, preferred_element_type=F32)
            m_ref[mi] = m_new

    def body(j, carry):
        kv_step(j, False)
        return carry

    lax.fori_loop(0, qi, body, 0)
    kv_step(qi, True)

    lv = lv_ref[...]
    lam = (jnp.exp(jnp.sum(lv[0:1] * lv[1:2], axis=-1, keepdims=True))
           - jnp.exp(jnp.sum(lv[2:3] * lv[3:4], axis=-1, keepdims=True)) + lam_init)
    o = acc_ref[0] / l_ref[0] - lam * (acc_ref[1] / l_ref[1])
    ms = jnp.mean(o * o, axis=-1, keepdims=True)
    o_ref[...] = ((o * lax.rsqrt(ms + EPS)) * subg_ref[...] * (1.0 - lam_init)).astype(BF16)


def _attention(z, lam_vecs, sub_g, lam_init, batch, seq):
    nq = seq // ATTN_TQ
    k_col0 = QK_WIDTH // HEAD_WIDTH
    v_col0 = 2 * QK_WIDTH // HEAD_WIDTH
    return pl.pallas_call(
        functools.partial(_attn_kernel, lam_init=lam_init),
        grid=(batch, DIFF_HEADS, nq),
        in_specs=[
            pl.BlockSpec((ATTN_TQ, HEAD_WIDTH), lambda b, h, i: (b * nq + i, h)),
            pl.BlockSpec((seq, HEAD_WIDTH), lambda b, h, i: (b, k_col0 + h)),
            pl.BlockSpec((seq, HEAD_WIDTH), lambda b, h, i: (b, v_col0 + h)),
            pl.BlockSpec((4, DIFF_HEAD_DIM), lambda b, h, i: (0, 0)),
            pl.BlockSpec((1, HEAD_WIDTH), lambda b, h, i: (0, 0)),
        ],
        out_specs=pl.BlockSpec((ATTN_TQ, HEAD_WIDTH), lambda b, h, i: (b * nq + i, h)),
        out_shape=jax.ShapeDtypeStruct((batch * seq, ATTN_WIDTH), BF16),
        scratch_shapes=[
            pltpu.VMEM((2, ATTN_TQ, 1), F32),
            pltpu.VMEM((2, ATTN_TQ, 1), F32),
            pltpu.VMEM((2, ATTN_TQ, HEAD_WIDTH), F32),
        ],
        compiler_params=_params(("parallel", "parallel", "arbitrary")),
        name="diff_attention",
    )(z, z, z, lam_vecs, sub_g)


def _mixers_kernel(up_ref, gb_ref, gc_ref, uc_ref, wp_ref, ps_ref, cw_ref, o_ref):
    seq = up_ref.shape[0]
    row = lax.broadcasted_iota(jnp.int32, (seq, LANES), 0)

    def delayed(x, k):
        return jnp.where(row >= k, pltpu.roll(x, k, axis=0), 0.0)

    for gi, w in enumerate(POOL_WINDOWS):
        cols = slice(gi * POOL_GROUP, (gi + 1) * POOL_GROUP)
        g = up_ref[:, cols].astype(F32)
        s, span = g, 1
        while span < w:
            s = s + delayed(s, span)
            span *= 2
        count = jnp.minimum(row + 1, w).astype(F32)
        pooled = s / count - g
        y = jnp.dot(pooled.astype(BF16), wp_ref[gi], preferred_element_type=F32) * ps_ref[:, cols]
        o_ref[:, cols] = y.astype(BF16)

    for c in range(CONV_WIDTH // LANES):
        cols = slice(c * LANES, (c + 1) * LANES)
        u = gc_ref[:, cols].astype(F32) * uc_ref[:, cols].astype(F32)
        y = (cw_ref[0:1, cols] * delayed(u, 2) + cw_ref[1:2, cols] * delayed(u, 1)) + cw_ref[2:3, cols] * u
        out = gb_ref[:, cols].astype(F32) * y
        o_ref[:, POOL_WIDTH + c * LANES:POOL_WIDTH + (c + 1) * LANES] = out.astype(BF16)


def _mixers(z, w_pool, pool_scale, conv_w, batch, seq):
    col0 = (2 * QK_WIDTH + ATTN_WIDTH) // POOL_WIDTH
    zspec = lambda off: pl.BlockSpec((seq, POOL_WIDTH), lambda b: (b, col0 + off))
    return pl.pallas_call(
        _mixers_kernel,
        grid=(batch,),
        in_specs=[
            zspec(0), zspec(1), zspec(2), zspec(3),
            pl.BlockSpec((len(POOL_WINDOWS), POOL_GROUP, POOL_GROUP), lambda b: (0, 0, 0)),
            pl.BlockSpec((1, POOL_WIDTH), lambda b: (0, 0)),
            pl.BlockSpec((CONV_TAPS, CONV_WIDTH), lambda b: (0, 0)),
        ],
        out_specs=pl.BlockSpec((seq, POOL_WIDTH + CONV_WIDTH), lambda b: (b, 0)),
        out_shape=jax.ShapeDtypeStruct((batch * seq, POOL_WIDTH + CONV_WIDTH), BF16),
        compiler_params=_params(("parallel",)),
        name="pool_conv_mixers",
    )(z, z, z, z, w_pool, pool_scale, conv_w)


def _out_proj_kernel(*refs, with_router):
    if with_router:
        attn_ref, mix_ref, x_ref, wo_ref, g_ref, rw_ref, xo_ref, h_ref, route_ref = refs
    else:
        attn_ref, mix_ref, x_ref, wo_ref, g_ref, xo_ref, h_ref = refs
    y = jnp.dot(attn_ref[...], wo_ref[:ATTN_WIDTH, :], preferred_element_type=F32)
    y = y + jnp.dot(mix_ref[...], wo_ref[ATTN_WIDTH:, :], preferred_element_type=F32)
    xn = x_ref[...] + y
    xo_ref[...] = xn
    ms = jnp.mean(xn * xn, axis=-1, keepdims=True)
    hf = xn * lax.rsqrt(ms + EPS) * g_ref[...]
    h_ref[...] = hf.astype(BF16)
    if with_router:
        logits = jnp.dot(hf, rw_ref[...], preferred_element_type=F32, precision=lax.Precision.HIGHEST)
        lane = lax.broadcasted_iota(jnp.int32, logits.shape, 1)
        logits = jnp.where(lane < N_EXPERTS, logits, -jnp.inf)
        v1 = jnp.max(logits, axis=-1, keepdims=True)
        i1 = jnp.min(jnp.where(logits == v1, lane, LANES), axis=-1, keepdims=True)
        rest = jnp.where(lane == i1, -jnp.inf, logits)
        v2 = jnp.max(rest, axis=-1, keepdims=True)
        i2 = jnp.min(jnp.where(rest == v2, lane, LANES), axis=-1, keepdims=True)
        e2 = jnp.exp(v2 - v1)
        g1 = 1.0 / (1.0 + e2)
        g2 = e2 / (1.0 + e2)
        route = jnp.where(lane == 0, g1, jnp.where(lane == 1, g2, jnp.where(
            lane == 2, i1.astype(F32), jnp.where(lane == 3, i2.astype(F32), 0.0))))
        route_ref[...] = route


def _out_proj(attn, mix, x, w_out, ffn_g, router_w=None):
    n = x.shape[0]
    with_router = router_w is not None
    row_spec = lambda width: pl.BlockSpec((OUT_TM, width), lambda i: (i, 0))
    in_specs = [row_spec(ATTN_WIDTH), row_spec(POOL_WIDTH + CONV_WIDTH), row_spec(D_MODEL),
                pl.BlockSpec((D_MODEL, D_MODEL), lambda i: (0, 0)),
                pl.BlockSpec((1, D_MODEL), lambda i: (0, 0))]
    out_specs = [row_spec(D_MODEL), row_spec(D_MODEL)]
    out_shape = [jax.ShapeDtypeStruct((n, D_MODEL), F32), jax.ShapeDtypeStruct((n, D_MODEL), BF16)]
    args = [attn, mix, x, w_out, ffn_g]
    if with_router:
        in_specs.append(pl.BlockSpec((D_MODEL, LANES), lambda i: (0, 0)))
        out_specs.append(row_spec(LANES))
        out_shape.append(jax.ShapeDtypeStruct((n, LANES), F32))
        args.append(router_w)
    return pl.pallas_call(
        functools.partial(_out_proj_kernel, with_router=with_router),
        grid=(n // OUT_TM,),
        in_specs=in_specs,
        out_specs=out_specs,
        out_shape=out_shape,
        compiler_params=_params(("parallel",)),
        name="out_proj_router" if with_router else "out_proj",
    )(*args)


def _ffn_kernel(te_ref, nt_ref, x_ref, wg_ref, wu_ref, wd_ref, side_ref, o_ref, acc_ref, *, n_f, gated):
    t = pl.program_id(0)
    f = pl.program_id(1)

    @pl.when(t < nt_ref[0])
    def _():
        @pl.when(f == 0)
        def _():
            acc_ref[...] = jnp.zeros(acc_ref.shape, F32)

        x = x_ref[...]
        g = jnp.dot(x, wg_ref[0].astype(BF16), preferred_element_type=F32)
        u = jnp.dot(x, wu_ref[0].astype(BF16), preferred_element_type=F32)
        a = (g * jax.nn.sigmoid(g) * u).astype(BF16)
        acc_ref[...] += jnp.dot(a, wd_ref[0].astype(BF16), preferred_element_type=F32)

        @pl.when(f == n_f - 1)
        def _():
            if gated:
                o_ref[...] = (side_ref[...] * acc_ref[...]).astype(o_ref.dtype)
            else:
                o_ref[...] = side_ref[...] + acc_ref[...]


def _grouped_ffn(tile_expert, n_tiles, x_rows, wg, wu, wd, side, *, gated, out_dtype):
    rows = x_rows.shape[0]
    n_t = rows // FFN_TM
    d_ff = wg.shape[-1]
    n_f = d_ff // FFN_TF

    def tile(t, nt):
        return jnp.minimum(t, nt[0] - 1)

    def hidden(t, f, nt):
        return jnp.where(t < nt[0], f, n_f - 1)

    side_width = 1 if gated else D_MODEL
    once = {} if gated else dict(pipeline_mode=pl.Buffered(1))
    grid_spec = pltpu.PrefetchScalarGridSpec(
        num_scalar_prefetch=2,
        grid=(n_t, n_f),
        in_specs=[
            pl.BlockSpec((FFN_TM, D_MODEL), lambda t, f, te, nt: (tile(t, nt), 0)),
            pl.BlockSpec((1, D_MODEL, FFN_TF), lambda t, f, te, nt: (te[t], 0, hidden(t, f, nt))),
            pl.BlockSpec((1, D_MODEL, FFN_TF), lambda t, f, te, nt: (te[t], 0, hidden(t, f, nt))),
            pl.BlockSpec((1, FFN_TF, D_MODEL), lambda t, f, te, nt: (te[t], hidden(t, f, nt), 0)),
            pl.BlockSpec((FFN_TM, side_width), lambda t, f, te, nt: (tile(t, nt), 0), **once),
        ],
        out_specs=pl.BlockSpec((FFN_TM, D_MODEL), lambda t, f, te, nt: (tile(t, nt), 0), **once),
        scratch_shapes=[pltpu.VMEM((FFN_TM, D_MODEL), F32)],
    )
    return pl.pallas_call(
        functools.partial(_ffn_kernel, n_f=n_f, gated=gated),
        grid_spec=grid_spec,
        out_shape=jax.ShapeDtypeStruct((rows, D_MODEL), out_dtype),
        compiler_params=_params(("arbitrary", "arbitrary")),
        name="moe_ffn" if gated else "dense_ffn",
    )(tile_expert, n_tiles, x_rows, wg, wu, wd, side)


def _moe_layer(x, h, route, wg, wu, wd):
    n = x.shape[0]
    n_slots = n * TOP_K + N_EXPERTS * FFN_TM
    n_slots -= n_slots % FFN_TM
    n_t = n_slots // FFN_TM
    gates = route[:, 0:TOP_K]
    idx = route[:, TOP_K:2 * TOP_K].astype(jnp.int32)
    onehot = jnp.sum((idx[:, :, None] == jnp.arange(N_EXPERTS)[None, None, :]).astype(jnp.int32), axis=1)
    before = jnp.cumsum(onehot, axis=0) - onehot
    counts = jnp.sum(onehot, axis=0)
    tiles_per = (counts + FFN_TM - 1) // FFN_TM
    tiles_end = jnp.cumsum(tiles_per)
    group_start = (tiles_end - tiles_per) * FFN_TM
    rank = jnp.take_along_axis(before, idx, axis=1)
    slot = group_start[idx] + rank
    n_tiles = tiles_end[-1:].astype(jnp.int32)
    t_ids = jnp.minimum(jnp.arange(n_t, dtype=jnp.int32), n_tiles[0] - 1)
    tile_expert = jnp.minimum(jnp.searchsorted(tiles_end, t_ids, side="right"), N_EXPERTS - 1).astype(jnp.int32)

    flat_slot = slot.reshape(-1)
    token_of_slot = jnp.zeros((n_slots,), jnp.int32).at[flat_slot].set(
        jnp.arange(n * TOP_K, dtype=jnp.int32) // TOP_K)
    gate_of_slot = jnp.zeros((n_slots,), F32).at[flat_slot].set(gates.reshape(-1))
    x_sorted = jnp.take(h, token_of_slot, axis=0)
    y = _grouped_ffn(tile_expert, n_tiles, x_sorted, wg, wu, wd, gate_of_slot[:, None],
                     gated=True, out_dtype=F32)
    return x + jnp.take(y, slot[:, 0], axis=0) + jnp.take(y, slot[:, 1], axis=0)


def kernel(x, positions, attn_norm_g, w_in, q_norm_g, k_norm_g, lambda_vecs, attn_out_norm_g, w_pool, pool_scale,
           conv_w, w_out, ffn_norm_g, dense_w_gate, dense_w_up, dense_w_down, router_w, moe_w_gate, moe_w_up,
           moe_w_down):
    batch, seq, d_model = x.shape
    depth = w_in.shape[0]
    n = batch * seq

    inv_freq = 1.0 / (ROPE_THETA ** (jnp.arange(0, DIFF_HEAD_DIM, 2, dtype=F32) / DIFF_HEAD_DIM))
    ang = positions.astype(F32)[..., None] * inv_freq
    cos = jnp.cos(ang).astype(x.dtype).reshape(n, DIFF_HEAD_DIM // 2)
    sin = jnp.sin(ang).astype(x.dtype).reshape(n, DIFF_HEAD_DIM // 2)
    cos2 = jnp.concatenate([cos, cos], axis=-1)
    sin2 = jnp.concatenate([-sin, sin], axis=-1)

    xf = x.reshape(n, d_model)
    dense_tiles = jnp.zeros((n // FFN_TM,), jnp.int32)
    dense_n_tiles = jnp.full((1,), n // FFN_TM, jnp.int32)

    for l in range(depth):
        lam_init = 0.8 - 0.6 * math.exp(-0.3 * l)
        qk_g = jnp.stack([q_norm_g[l], k_norm_g[l]])[:, None, :]
        z = _in_proj(xf, attn_norm_g[l][None, :], w_in[l].astype(BF16), cos2, sin2, qk_g)
        attn = _attention(z, lambda_vecs[l], attn_out_norm_g[l][None, :], lam_init, batch, seq)
        mix = _mixers(z, w_pool[l].astype(BF16), pool_scale[l][None, :], conv_w[l], batch, seq)
        j = l // 2
        if l % 2 == 0:
            xf, h = _out_proj(attn, mix, xf, w_out[l].astype(BF16), ffn_norm_g[l][None, :])
            xf = _grouped_ffn(dense_tiles, dense_n_tiles, h, dense_w_gate[j][None], dense_w_up[j][None],
                              dense_w_down[j][None], xf, gated=False, out_dtype=F32)
        else:
            rw = jnp.pad(router_w[j], ((0, 0), (0, LANES - N_EXPERTS)))
            xf, h, route = _out_proj(attn, mix, xf, w_out[l].astype(BF16), ffn_norm_g[l][None, :], rw)
            xf = _moe_layer(xf, h, route, moe_w_gate[j], moe_w_up[j], moe_w_down[j])
    return xf.reshape(batch, seq, d_model)
```

```python
import functools
import math

import jax
import jax.numpy as jnp
from jax import lax
from jax.experimental import pallas as pl
from jax.experimental.pallas import tpu as pltpu

D_MODEL = 2048
DIFF_HEAD_DIM = 128
DIFF_HEADS = 4
HEAD_WIDTH = 2 * DIFF_HEAD_DIM
QK_WIDTH = DIFF_HEADS * HEAD_WIDTH
ATTN_WIDTH = QK_WIDTH
POOL_WINDOWS = (2, 4, 8, 16)
POOL_GROUP = 128
POOL_WIDTH = 512
CONV_WIDTH = 512
CONV_TAPS = 3
IN_WIDTH = 5120
ROPE_THETA = 10000.0
N_EXPERTS = 8
TOP_K = 2
EPS = 1e-6
NEG_INF = -1e30
LANES = 128

BF16 = jnp.bfloat16
F32 = jnp.float32

VMEM_LIMIT_BYTES = 56 * 1024 * 1024

IN_TM = 1024
IN_TN = 1024
ATTN_TQ = 512
OUT_TM = 512
FFN_TM = 1024
FFN_TF = 256


def _params(semantics):
    return pltpu.CompilerParams(dimension_semantics=semantics, vmem_limit_bytes=VMEM_LIMIT_BYTES)


def _in_proj_kernel(x_ref, g_ref, w_ref, cos_ref, sin_ref, qkg_ref, z_ref, h_ref):
    j = pl.program_id(1)

    @pl.when(j == 0)
    def _():
        x = x_ref[...]
        ms = jnp.mean(x * x, axis=-1, keepdims=True)
        h_ref[...] = (x * lax.rsqrt(ms + EPS) * g_ref[...]).astype(BF16)

    acc = jnp.dot(h_ref[...], w_ref[...], preferred_element_type=F32)

    @pl.when(j < 2)
    def _():
        gain = qkg_ref[j]
        scale = jnp.where(j == 0, DIFF_HEAD_DIM ** -0.5, 1.0).astype(F32)
        cos = cos_ref[...]
        sin = sin_ref[...]
        for c in range(IN_TN // DIFF_HEAD_DIM):
            zc = acc[:, c * DIFF_HEAD_DIM:(c + 1) * DIFF_HEAD_DIM]
            ms = jnp.mean(zc * zc, axis=-1, keepdims=True)
            zn = zc * lax.rsqrt(ms + EPS) * gain
            rot = zn * cos + pltpu.roll(zn, DIFF_HEAD_DIM // 2, axis=1) * sin
            z_ref[:, c * DIFF_HEAD_DIM:(c + 1) * DIFF_HEAD_DIM] = (rot * scale).astype(BF16)

    @pl.when(j >= 2)
    def _():
        z_ref[...] = acc.astype(BF16)


def _in_proj(x, norm_g, w_in, cos2, sin2, qk_g):
    n = x.shape[0]
    grid = (n // IN_TM, IN_WIDTH // IN_TN)
    return pl.pallas_call(
        _in_proj_kernel,
        grid=grid,
        in_specs=[
            pl.BlockSpec((IN_TM, D_MODEL), lambda i, j: (i, 0)),
            pl.BlockSpec((1, D_MODEL), lambda i, j: (0, 0)),
            pl.BlockSpec((D_MODEL, IN_TN), lambda i, j: (0, j)),
            pl.BlockSpec((IN_TM, DIFF_HEAD_DIM), lambda i, j: (i, 0)),
            pl.BlockSpec((IN_TM, DIFF_HEAD_DIM), lambda i, j: (i, 0)),
            pl.BlockSpec((2, 1, DIFF_HEAD_DIM), lambda i, j: (0, 0, 0)),
        ],
        out_specs=pl.BlockSpec((IN_TM, IN_TN), lambda i, j: (i, j)),
        out_shape=jax.ShapeDtypeStruct((n, IN_WIDTH), BF16),
        scratch_shapes=[pltpu.VMEM((IN_TM, D_MODEL), BF16)],
        compiler_params=_params(("parallel", "arbitrary")),
        name="in_proj",
    )(x, norm_g, w_in, cos2, sin2, qk_g)


def _attn_kernel(q_ref, k_ref, v_ref, lv_ref, subg_ref, o_ref, *, lam_init):
    t, d = ATTN_TQ, DIFF_HEAD_DIM
    seq = q_ref.shape[0]
    lv = lv_ref[...]
    lam = (jnp.exp(jnp.sum(lv[0:1] * lv[1:2], axis=-1, keepdims=True))
           - jnp.exp(jnp.sum(lv[2:3] * lv[3:4], axis=-1, keepdims=True)) + lam_init)
    causal = (lax.broadcasted_iota(jnp.int32, (t, t), 1) <= lax.broadcasted_iota(jnp.int32, (t, t), 0))
    nt_dims = (((1,), (1,)), ((), ()))

    for i in range(seq // t):
        lo = i * t
        a_diag, a_past = None, None
        for mi in range(2):
            cols = slice(mi * d, (mi + 1) * d)
            qm = q_ref[lo:lo + t, cols]
            s_diag = lax.dot_general(qm, k_ref[lo:lo + t, cols], nt_dims, preferred_element_type=F32)
            s_diag = jnp.where(causal, s_diag, NEG_INF)
            mx = jnp.max(s_diag, axis=-1, keepdims=True)
            if i > 0:
                s_past = lax.dot_general(qm, k_ref[0:lo, cols], nt_dims, preferred_element_type=F32)
                mx = jnp.maximum(mx, jnp.max(s_past, axis=-1, keepdims=True))
            p_diag = jnp.exp(s_diag - mx)
            denom = jnp.sum(p_diag, axis=-1, keepdims=True)
            if i > 0:
                p_past = jnp.exp(s_past - mx)
                denom = denom + jnp.sum(p_past, axis=-1, keepdims=True)
            weight = 1.0 / denom if mi == 0 else lam / denom
            if mi == 0:
                a_diag = p_diag * weight
                a_past = p_past * weight if i > 0 else None
            else:
                a_diag = a_diag - p_diag * weight
                a_past = a_past - p_past * weight if i > 0 else None
        o = jnp.dot(a_diag.astype(BF16), v_ref[lo:lo + t, :], preferred_element_type=F32)
        if i > 0:
            o = o + jnp.dot(a_past.astype(BF16), v_ref[0:lo, :], preferred_element_type=F32)
        ms = jnp.mean(o * o, axis=-1, keepdims=True)
        o_ref[lo:lo + t, :] = ((o * lax.rsqrt(ms + EPS)) * subg_ref[...] * (1.0 - lam_init)).astype(BF16)


def _attention(z, lam_vecs, sub_g, lam_init, batch, seq):
    k_col0 = QK_WIDTH // HEAD_WIDTH
    v_col0 = 2 * QK_WIDTH // HEAD_WIDTH
    return pl.pallas_call(
        functools.partial(_attn_kernel, lam_init=lam_init),
        grid=(batch, DIFF_HEADS),
        in_specs=[
            pl.BlockSpec((seq, HEAD_WIDTH), lambda b, h: (b, h)),
            pl.BlockSpec((seq, HEAD_WIDTH), lambda b, h: (b, k_col0 + h)),
            pl.BlockSpec((seq, HEAD_WIDTH), lambda b, h: (b, v_col0 + h)),
            pl.BlockSpec((4, DIFF_HEAD_DIM), lambda b, h: (0, 0)),
            pl.BlockSpec((1, HEAD_WIDTH), lambda b, h: (0, 0)),
        ],
        out_specs=pl.BlockSpec((seq, HEAD_WIDTH), lambda b, h: (b, h)),
        out_shape=jax.ShapeDtypeStruct((batch * seq, ATTN_WIDTH), BF16),
        compiler_params=_params(("parallel", "parallel")),
        name="diff_attention",
    )(z, z, z, lam_vecs, sub_g)


def _mixers_kernel(up_ref, gb_ref, gc_ref, uc_ref, wp_ref, ps_ref, cw_ref, o_ref):
    seq = up_ref.shape[0]
    row = lax.broadcasted_iota(jnp.int32, (seq, LANES), 0)

    def delayed(x, k):
        return jnp.where(row >= k, pltpu.roll(x, k, axis=0), 0.0)

    for gi, w in enumerate(POOL_WINDOWS):
        cols = slice(gi * POOL_GROUP, (gi + 1) * POOL_GROUP)
        g = up_ref[:, cols].astype(F32)
        s, span = g, 1
        while span < w:
            s = s + delayed(s, span)
            span *= 2
        count = jnp.minimum(row + 1, w).astype(F32)
        pooled = s / count - g
        y = jnp.dot(pooled.astype(BF16), wp_ref[gi], preferred_element_type=F32) * ps_ref[:, cols]
        o_ref[:, cols] = y.astype(BF16)

    for c in range(CONV_WIDTH // LANES):
        cols = slice(c * LANES, (c + 1) * LANES)
        u = gc_ref[:, cols].astype(F32) * uc_ref[:, cols].astype(F32)
        y = (cw_ref[0:1, cols] * delayed(u, 2) + cw_ref[1:2, cols] * delayed(u, 1)) + cw_ref[2:3, cols] * u
        out = gb_ref[:, cols].astype(F32) * y
        o_ref[:, POOL_WIDTH + c * LANES:POOL_WIDTH + (c + 1) * LANES] = out.astype(BF16)


def _mixers(z, w_pool, pool_scale, conv_w, batch, seq):
    col0 = (2 * QK_WIDTH + ATTN_WIDTH) // POOL_WIDTH
    zspec = lambda off: pl.BlockSpec((seq, POOL_WIDTH), lambda b: (b, col0 + off))
    return pl.pallas_call(
        _mixers_kernel,
        grid=(batch,),
        in_specs=[
            zspec(0), zspec(1), zspec(2), zspec(3),
            pl.BlockSpec((len(POOL_WINDOWS), POOL_GROUP, POOL_GROUP), lambda b: (0, 0, 0)),
            pl.BlockSpec((1, POOL_WIDTH), lambda b: (0, 0)),
            pl.BlockSpec((CONV_TAPS, CONV_WIDTH), lambda b: (0, 0)),
        ],
        out_specs=pl.BlockSpec((seq, POOL_WIDTH + CONV_WIDTH), lambda b: (b, 0)),
        out_shape=jax.ShapeDtypeStruct((batch * seq, POOL_WIDTH + CONV_WIDTH), BF16),
        compiler_params=_params(("parallel",)),
        name="pool_conv_mixers",
    )(z, z, z, z, w_pool, pool_scale, conv_w)


def _out_proj_kernel(*refs, with_router):
    if with_router:
        attn_ref, mix_ref, x_ref, wo_ref, g_ref, rw_ref, xo_ref, h_ref, route_ref = refs
    else:
        attn_ref, mix_ref, x_ref, wo_ref, g_ref, xo_ref, h_ref = refs
    y = jnp.dot(attn_ref[...], wo_ref[:ATTN_WIDTH, :], preferred_element_type=F32)
    y = y + jnp.dot(mix_ref[...], wo_ref[ATTN_WIDTH:, :], preferred_element_type=F32)
    xn = x_ref[...] + y
    xo_ref[...] = xn
    ms = jnp.mean(xn * xn, axis=-1, keepdims=True)
    hf = xn * lax.rsqrt(ms + EPS) * g_ref[...]
    h_ref[...] = hf.astype(BF16)
    if with_router:
        h_hi = hf.astype(BF16)
        h_lo = (hf - h_hi.astype(F32)).astype(BF16)
        rw = rw_ref[...]
        r_hi = rw.astype(BF16)
        r_lo = (rw - r_hi.astype(F32)).astype(BF16)
        logits = (jnp.dot(h_hi, r_hi, preferred_element_type=F32) + jnp.dot(h_hi, r_lo, preferred_element_type=F32)
                  + jnp.dot(h_lo, r_hi, preferred_element_type=F32))
        lane = lax.broadcasted_iota(jnp.int32, logits.shape, 1)
        logits = jnp.where(lane < N_EXPERTS, logits, -jnp.inf)
        v1 = jnp.max(logits, axis=-1, keepdims=True)
        i1 = jnp.min(jnp.where(logits == v1, lane, LANES), axis=-1, keepdims=True)
        rest = jnp.where(lane == i1, -jnp.inf, logits)
        v2 = jnp.max(rest, axis=-1, keepdims=True)
        i2 = jnp.min(jnp.where(rest == v2, lane, LANES), axis=-1, keepdims=True)
        e2 = jnp.exp(v2 - v1)
        g1 = 1.0 / (1.0 + e2)
        g2 = e2 / (1.0 + e2)
        route = jnp.where(lane == 0, g1, jnp.where(lane == 1, g2, jnp.where(
            lane == 2, i1.astype(F32), jnp.where(lane == 3, i2.astype(F32), 0.0))))
        route_ref[...] = route


def _out_proj(attn, mix, x, w_out, ffn_g, router_w=None):
    n = x.shape[0]
    with_router = router_w is not None
    row_spec = lambda width: pl.BlockSpec((OUT_TM, width), lambda i: (i, 0))
    in_specs = [row_spec(ATTN_WIDTH), row_spec(POOL_WIDTH + CONV_WIDTH), row_spec(D_MODEL),
                pl.BlockSpec((D_MODEL, D_MODEL), lambda i: (0, 0)),
                pl.BlockSpec((1, D_MODEL), lambda i: (0, 0))]
    out_specs = [row_spec(D_MODEL), row_spec(D_MODEL)]
    out_shape = [jax.ShapeDtypeStruct((n, D_MODEL), F32), jax.ShapeDtypeStruct((n, D_MODEL), BF16)]
    args = [attn, mix, x, w_out, ffn_g]
    if with_router:
        in_specs.append(pl.BlockSpec((D_MODEL, LANES), lambda i: (0, 0)))
        out_specs.append(row_spec(LANES))
        out_shape.append(jax.ShapeDtypeStruct((n, LANES), F32))
        args.append(router_w)
    return pl.pallas_call(
        functools.partial(_out_proj_kernel, with_router=with_router),
        grid=(n // OUT_TM,),
        in_specs=in_specs,
        out_specs=out_specs,
        out_shape=out_shape,
        compiler_params=_params(("parallel",)),
        name="out_proj_router" if with_router else "out_proj",
    )(*args)


def _ffn_kernel(te_ref, nt_ref, x_ref, wg_ref, wu_ref, wd_ref, side_ref, o_ref, acc_ref, *, n_f, gated):
    t = pl.program_id(0)
    f = pl.program_id(1)

    @pl.when(t < nt_ref[0])
    def _():
        @pl.when(f == 0)
        def _():
            acc_ref[...] = jnp.zeros(acc_ref.shape, F32)

        x = x_ref[...]
        g = jnp.dot(x, wg_ref[0].astype(BF16), preferred_element_type=F32)
        u = jnp.dot(x, wu_ref[0].astype(BF16), preferred_element_type=F32)
        a = (g * jax.nn.sigmoid(g) * u).astype(BF16)
        acc_ref[...] += jnp.dot(a, wd_ref[0].astype(BF16), preferred_element_type=F32)

        @pl.when(f == n_f - 1)
        def _():
            if gated:
                o_ref[...] = (side_ref[...] * acc_ref[...]).astype(o_ref.dtype)
            else:
                o_ref[...] = side_ref[...] + acc_ref[...]


def _grouped_ffn(tile_expert, n_tiles, x_rows, wg, wu, wd, side, *, gated, out_dtype):
    rows = x_rows.shape[0]
    n_t = rows // FFN_TM
    d_ff = wg.shape[-1]
    n_f = d_ff // FFN_TF

    def tile(t, nt):
        return jnp.minimum(t, nt[0] - 1)

    def hidden(t, f, nt):
        return jnp.where(t < nt[0], f, n_f - 1)

    side_width = 1 if gated else D_MODEL
    once = {} if gated else dict(pipeline_mode=pl.Buffered(1))
    grid_spec = pltpu.PrefetchScalarGridSpec(
        num_scalar_prefetch=2,
        grid=(n_t, n_f),
        in_specs=[
            pl.BlockSpec((FFN_TM, D_MODEL), lambda t, f, te, nt: (tile(t, nt), 0)),
            pl.BlockSpec((1, D_MODEL, FFN_TF), lambda t, f, te, nt: (te[t], 0, hidden(t, f, nt))),
            pl.BlockSpec((1, D_MODEL, FFN_TF), lambda t, f, te, nt: (te[t], 0, hidden(t, f, nt))),
            pl.BlockSpec((1, FFN_TF, D_MODEL), lambda t, f, te, nt: (te[t], hidden(t, f, nt), 0)),
            pl.BlockSpec((FFN_TM, side_width), lambda t, f, te, nt: (tile(t, nt), 0), **once),
        ],
        out_specs=pl.BlockSpec((FFN_TM, D_MODEL), lambda t, f, te, nt: (tile(t, nt), 0), **once),
        scratch_shapes=[pltpu.VMEM((FFN_TM, D_MODEL), F32)],
    )
    return pl.pallas_call(
        functools.partial(_ffn_kernel, n_f=n_f, gated=gated),
        grid_spec=grid_spec,
        out_shape=jax.ShapeDtypeStruct((rows, D_MODEL), out_dtype),
        compiler_params=_params(("arbitrary", "arbitrary")),
        name="moe_ffn" if gated else "dense_ffn",
    )(tile_expert, n_tiles, x_rows, wg, wu, wd, side)


def _moe_layer(x, h, route, wg, wu, wd):
    n = x.shape[0]
    n_slots = n * TOP_K + N_EXPERTS * FFN_TM
    n_slots -= n_slots % FFN_TM
    n_t = n_slots // FFN_TM
    gates = route[:, 0:TOP_K]
    idx = route[:, TOP_K:2 * TOP_K].astype(jnp.int32)
    onehot = jnp.sum((idx[:, :, None] == jnp.arange(N_EXPERTS)[None, None, :]).astype(jnp.int32), axis=1)
    before = jnp.cumsum(onehot, axis=0) - onehot
    counts = jnp.sum(onehot, axis=0)
    tiles_per = (counts + FFN_TM - 1) // FFN_TM
    tiles_end = jnp.cumsum(tiles_per)
    group_start = (tiles_end - tiles_per) * FFN_TM
    rank = jnp.take_along_axis(before, idx, axis=1)
    slot = group_start[idx] + rank
    n_tiles = tiles_end[-1:].astype(jnp.int32)
    t_ids = jnp.minimum(jnp.arange(n_t, dtype=jnp.int32), n_tiles[0] - 1)
    tile_expert = jnp.minimum(jnp.sum((t_ids[:, None] >= tiles_end[None, :]).astype(jnp.int32), axis=1),
                              N_EXPERTS - 1)

    flat_slot = slot.reshape(-1)
    token_of_slot = jnp.zeros((n_slots,), jnp.int32).at[flat_slot].set(
        jnp.arange(n * TOP_K, dtype=jnp.int32) // TOP_K)
    gate_of_slot = jnp.zeros((n_slots,), F32).at[flat_slot].set(gates.reshape(-1))
    x_sorted = jnp.take(h, token_of_slot, axis=0)
    y = _grouped_ffn(tile_expert, n_tiles, x_sorted, wg, wu, wd, gate_of_slot[:, None],
                     gated=True, out_dtype=F32)
    return x + jnp.take(y, slot[:, 0], axis=0) + jnp.take(y, slot[:, 1], axis=0)


def kernel(x, positions, attn_norm_g, w_in, q_norm_g, k_norm_g, lambda_vecs, attn_out_norm_g, w_pool, pool_scale,
           conv_w, w_out, ffn_norm_g, dense_w_gate, dense_w_up, dense_w_down, router_w, moe_w_gate, moe_w_up,
           moe_w_down):
    batch, seq, d_model = x.shape
    depth = w_in.shape[0]
    n = batch * seq

    inv_freq = 1.0 / (ROPE_THETA ** (jnp.arange(0, DIFF_HEAD_DIM, 2, dtype=F32) / DIFF_HEAD_DIM))
    ang = positions.astype(F32)[..., None] * inv_freq
    cos = jnp.cos(ang).astype(x.dtype).reshape(n, DIFF_HEAD_DIM // 2)
    sin = jnp.sin(ang).astype(x.dtype).reshape(n, DIFF_HEAD_DIM // 2)
    cos2 = jnp.concatenate([cos, cos], axis=-1)
    sin2 = jnp.concatenate([-sin, sin], axis=-1)

    xf = x.reshape(n, d_model)
    dense_tiles = jnp.zeros((n // FFN_TM,), jnp.int32)
    dense_n_tiles = jnp.full((1,), n // FFN_TM, jnp.int32)

    for l in range(depth):
        lam_init = 0.8 - 0.6 * math.exp(-0.3 * l)
        qk_g = jnp.stack([q_norm_g[l], k_norm_g[l]])[:, None, :]
        z = _in_proj(xf, attn_norm_g[l][None, :], w_in[l].astype(BF16), cos2, sin2, qk_g)
        attn = _attention(z, lambda_vecs[l], attn_out_norm_g[l][None, :], lam_init, batch, seq)
        mix = _mixers(z, w_pool[l].astype(BF16), pool_scale[l][None, :], conv_w[l], batch, seq)
        j = l // 2
        if l % 2 == 0:
            xf, h = _out_proj(attn, mix, xf, w_out[l].astype(BF16), ffn_norm_g[l][None, :])
            xf = _grouped_ffn(dense_tiles, dense_n_tiles, h, dense_w_gate[j][None], dense_w_up[j][None],
                              dense_w_down[j][None], xf, gated=False, out_dtype=F32)
        else:
            rw = jnp.pad(router_w[j], ((0, 0), (0, LANES - N_EXPERTS)))
            xf, h, route = _out_proj(attn, mix, xf, w_out[l].astype(BF16), ffn_norm_g[l][None, :], rw)
            xf = _moe_layer(xf, h, route, moe_w_gate[j], moe_w_up[j], moe_w_down[j])
    return xf.reshape(batch, seq, d_model)
```

```python
import functools
import math

import jax
import jax.numpy as jnp
from jax import lax
from jax.experimental import pallas as pl
from jax.experimental.pallas import tpu as pltpu

D_MODEL = 2048
DIFF_HEAD_DIM = 128
DIFF_HEADS = 4
HEAD_WIDTH = 2 * DIFF_HEAD_DIM
QK_WIDTH = DIFF_HEADS * HEAD_WIDTH
ATTN_WIDTH = QK_WIDTH
POOL_WINDOWS = (2, 4, 8, 16)
POOL_GROUP = 128
POOL_WIDTH = 512
CONV_WIDTH = 512
CONV_TAPS = 3
IN_WIDTH = 5120
ROPE_THETA = 10000.0
N_EXPERTS = 8
TOP_K = 2
EPS = 1e-6
NEG_INF = -1e30
LANES = 128

BF16 = jnp.bfloat16
F32 = jnp.float32

VMEM_LIMIT_BYTES = 56 * 1024 * 1024

IN_TM = 1024
IN_TN = 1024
ATTN_TQ = 512
OUT_TM = 512
FFN_TM = 1024
FFN_TF = 256
PERM_TM = 512


def _params(semantics):
    return pltpu.CompilerParams(dimension_semantics=semantics, vmem_limit_bytes=VMEM_LIMIT_BYTES)


def _in_proj_kernel(x_ref, g_ref, w_ref, cos_ref, sin_ref, qkg_ref, z_ref, h_ref):
    j = pl.program_id(1)

    @pl.when(j == 0)
    def _():
        x = x_ref[...]
        ms = jnp.mean(x * x, axis=-1, keepdims=True)
        h_ref[...] = (x * lax.rsqrt(ms + EPS) * g_ref[...]).astype(BF16)

    acc = jnp.dot(h_ref[...], w_ref[...], preferred_element_type=F32)

    @pl.when(j < 2)
    def _():
        gain = qkg_ref[j]
        scale = jnp.where(j == 0, DIFF_HEAD_DIM ** -0.5, 1.0).astype(F32)
        cos = cos_ref[...]
        sin = sin_ref[...]
        for c in range(IN_TN // DIFF_HEAD_DIM):
            zc = acc[:, c * DIFF_HEAD_DIM:(c + 1) * DIFF_HEAD_DIM]
            ms = jnp.mean(zc * zc, axis=-1, keepdims=True)
            zn = zc * lax.rsqrt(ms + EPS) * gain
            rot = zn * cos + pltpu.roll(zn, DIFF_HEAD_DIM // 2, axis=1) * sin
            z_ref[:, c * DIFF_HEAD_DIM:(c + 1) * DIFF_HEAD_DIM] = (rot * scale).astype(BF16)

    @pl.when(j >= 2)
    def _():
        z_ref[...] = acc.astype(BF16)


def _in_proj(x, norm_g, w_in, cos2, sin2, qk_g):
    n = x.shape[0]
    grid = (n // IN_TM, IN_WIDTH // IN_TN)
    return pl.pallas_call(
        _in_proj_kernel,
        grid=grid,
        in_specs=[
            pl.BlockSpec((IN_TM, D_MODEL), lambda i, j: (i, 0)),
            pl.BlockSpec((1, D_MODEL), lambda i, j: (0, 0)),
            pl.BlockSpec((D_MODEL, IN_TN), lambda i, j: (0, j)),
            pl.BlockSpec((IN_TM, DIFF_HEAD_DIM), lambda i, j: (i, 0)),
            pl.BlockSpec((IN_TM, DIFF_HEAD_DIM), lambda i, j: (i, 0)),
            pl.BlockSpec((2, 1, DIFF_HEAD_DIM), lambda i, j: (0, 0, 0)),
        ],
        out_specs=pl.BlockSpec((IN_TM, IN_TN), lambda i, j: (i, j)),
        out_shape=jax.ShapeDtypeStruct((n, IN_WIDTH), BF16),
        scratch_shapes=[pltpu.VMEM((IN_TM, D_MODEL), BF16)],
        compiler_params=_params(("parallel", "arbitrary")),
        name="in_proj",
    )(x, norm_g, w_in, cos2, sin2, qk_g)


def _attn_kernel(q_ref, k_ref, v_ref, lv_ref, subg_ref, o_ref, *, lam_init):
    t, d = ATTN_TQ, DIFF_HEAD_DIM
    seq = q_ref.shape[0]
    lv = lv_ref[...]
    lam = (jnp.exp(jnp.sum(lv[0:1] * lv[1:2], axis=-1, keepdims=True))
           - jnp.exp(jnp.sum(lv[2:3] * lv[3:4], axis=-1, keepdims=True)) + lam_init)
    causal = (lax.broadcasted_iota(jnp.int32, (t, t), 1) <= lax.broadcasted_iota(jnp.int32, (t, t), 0))
    nt_dims = (((1,), (1,)), ((), ()))

    for i in range(seq // t):
        lo = i * t
        a_diag, a_past = None, None
        for mi in range(2):
            cols = slice(mi * d, (mi + 1) * d)
            qm = q_ref[lo:lo + t, cols]
            s_diag = lax.dot_general(qm, k_ref[lo:lo + t, cols], nt_dims, preferred_element_type=F32)
            s_diag = jnp.where(causal, s_diag, NEG_INF)
            mx = jnp.max(s_diag, axis=-1, keepdims=True)
            if i > 0:
                s_past = lax.dot_general(qm, k_ref[0:lo, cols], nt_dims, preferred_element_type=F32)
                mx = jnp.maximum(mx, jnp.max(s_past, axis=-1, keepdims=True))
            p_diag = jnp.exp(s_diag - mx)
            denom = jnp.sum(p_diag, axis=-1, keepdims=True)
            if i > 0:
                p_past = jnp.exp(s_past - mx)
                denom = denom + jnp.sum(p_past, axis=-1, keepdims=True)
            weight = 1.0 / denom if mi == 0 else lam / denom
            if mi == 0:
                a_diag = p_diag * weight
                a_past = p_past * weight if i > 0 else None
            else:
                a_diag = a_diag - p_diag * weight
                a_past = a_past - p_past * weight if i > 0 else None
        o = jnp.dot(a_diag.astype(BF16), v_ref[lo:lo + t, :], preferred_element_type=F32)
        if i > 0:
            o = o + jnp.dot(a_past.astype(BF16), v_ref[0:lo, :], preferred_element_type=F32)
        ms = jnp.mean(o * o, axis=-1, keepdims=True)
        o_ref[lo:lo + t, :] = ((o * lax.rsqrt(ms + EPS)) * subg_ref[...] * (1.0 - lam_init)).astype(BF16)


def _attention(z, lam_vecs, sub_g, lam_init, batch, seq):
    k_col0 = QK_WIDTH // HEAD_WIDTH
    v_col0 = 2 * QK_WIDTH // HEAD_WIDTH
    return pl.pallas_call(
        functools.partial(_attn_kernel, lam_init=lam_init),
        grid=(batch, DIFF_HEADS),
        in_specs=[
            pl.BlockSpec((seq, HEAD_WIDTH), lambda b, h: (b, h)),
            pl.BlockSpec((seq, HEAD_WIDTH), lambda b, h: (b, k_col0 + h)),
            pl.BlockSpec((seq, HEAD_WIDTH), lambda b, h: (b, v_col0 + h)),
            pl.BlockSpec((4, DIFF_HEAD_DIM), lambda b, h: (0, 0)),
            pl.BlockSpec((1, HEAD_WIDTH), lambda b, h: (0, 0)),
        ],
        out_specs=pl.BlockSpec((seq, HEAD_WIDTH), lambda b, h: (b, h)),
        out_shape=jax.ShapeDtypeStruct((batch * seq, ATTN_WIDTH), BF16),
        compiler_params=_params(("parallel", "parallel")),
        name="diff_attention",
    )(z, z, z, lam_vecs, sub_g)


def _mixers_kernel(up_ref, gb_ref, gc_ref, uc_ref, wp_ref, ps_ref, cw_ref, o_ref):
    seq = up_ref.shape[0]
    row = lax.broadcasted_iota(jnp.int32, (seq, LANES), 0)

    def delayed(x, k):
        return jnp.where(row >= k, pltpu.roll(x, k, axis=0), 0.0)

    for gi, w in enumerate(POOL_WINDOWS):
        cols = slice(gi * POOL_GROUP, (gi + 1) * POOL_GROUP)
        g = up_ref[:, cols].astype(F32)
        s, span = g, 1
        while span < w:
            s = s + delayed(s, span)
            span *= 2
        count = jnp.minimum(row + 1, w).astype(F32)
        pooled = s / count - g
        y = jnp.dot(pooled.astype(BF16), wp_ref[gi], preferred_element_type=F32) * ps_ref[:, cols]
        o_ref[:, cols] = y.astype(BF16)

    for c in range(CONV_WIDTH // LANES):
        cols = slice(c * LANES, (c + 1) * LANES)
        u = gc_ref[:, cols].astype(F32) * uc_ref[:, cols].astype(F32)
        y = (cw_ref[0:1, cols] * delayed(u, 2) + cw_ref[1:2, cols] * delayed(u, 1)) + cw_ref[2:3, cols] * u
        out = gb_ref[:, cols].astype(F32) * y
        o_ref[:, POOL_WIDTH + c * LANES:POOL_WIDTH + (c + 1) * LANES] = out.astype(BF16)


def _mixers(z, w_pool, pool_scale, conv_w, batch, seq):
    col0 = (2 * QK_WIDTH + ATTN_WIDTH) // POOL_WIDTH
    zspec = lambda off: pl.BlockSpec((seq, POOL_WIDTH), lambda b: (b, col0 + off))
    return pl.pallas_call(
        _mixers_kernel,
        grid=(batch,),
        in_specs=[
            zspec(0), zspec(1), zspec(2), zspec(3),
            pl.BlockSpec((len(POOL_WINDOWS), POOL_GROUP, POOL_GROUP), lambda b: (0, 0, 0)),
            pl.BlockSpec((1, POOL_WIDTH), lambda b: (0, 0)),
            pl.BlockSpec((CONV_TAPS, CONV_WIDTH), lambda b: (0, 0)),
        ],
        out_specs=pl.BlockSpec((seq, POOL_WIDTH + CONV_WIDTH), lambda b: (b, 0)),
        out_shape=jax.ShapeDtypeStruct((batch * seq, POOL_WIDTH + CONV_WIDTH), BF16),
        compiler_params=_params(("parallel",)),
        name="pool_conv_mixers",
    )(z, z, z, z, w_pool, pool_scale, conv_w)


ROUTE_GATE, ROUTE_EXPERT, ROUTE_RANK = 0, 2, 4


def _out_proj_kernel(*refs, with_router):
    if with_router:
        attn_ref, mix_ref, x_ref, wo_ref, g_ref, rw_ref, xo_ref, h_ref, route_ref, counts_ref, run_ref = refs
    else:
        attn_ref, mix_ref, x_ref, wo_ref, g_ref, xo_ref, h_ref = refs
    y = jnp.dot(attn_ref[...], wo_ref[:ATTN_WIDTH, :], preferred_element_type=F32)
    y = y + jnp.dot(mix_ref[...], wo_ref[ATTN_WIDTH:, :], preferred_element_type=F32)
    xn = x_ref[...] + y
    xo_ref[...] = xn
    ms = jnp.mean(xn * xn, axis=-1, keepdims=True)
    hf = xn * lax.rsqrt(ms + EPS) * g_ref[...]
    h_ref[...] = hf.astype(h_ref.dtype)
    if with_router:
        h_hi = hf.astype(BF16)
        h_lo = (hf - h_hi.astype(F32)).astype(BF16)
        rw = rw_ref[...]
        r_hi = rw.astype(BF16)
        r_lo = (rw - r_hi.astype(F32)).astype(BF16)
        logits = (jnp.dot(h_hi, r_hi, preferred_element_type=F32) + jnp.dot(h_hi, r_lo, preferred_element_type=F32)
                  + jnp.dot(h_lo, r_hi, preferred_element_type=F32))
        tm = logits.shape[0]
        lane = lax.broadcasted_iota(jnp.int32, logits.shape, 1)
        logits = jnp.where(lane < N_EXPERTS, logits, -jnp.inf)
        v1 = jnp.max(logits, axis=-1, keepdims=True)
        i1 = jnp.min(jnp.where(logits == v1, lane, LANES), axis=-1, keepdims=True)
        rest = jnp.where(lane == i1, -jnp.inf, logits)
        v2 = jnp.max(rest, axis=-1, keepdims=True)
        i2 = jnp.min(jnp.where(rest == v2, lane, LANES), axis=-1, keepdims=True)
        e2 = jnp.exp(v2 - v1)
        g1 = 1.0 / (1.0 + e2)
        g2 = e2 / (1.0 + e2)

        @pl.when(pl.program_id(0) == 0)
        def _():
            run_ref[...] = jnp.zeros(run_ref.shape, F32)

        sel1 = lane == i1
        sel2 = lane == i2
        chosen = jnp.where(sel1 | sel2, 1.0, 0.0)
        earlier = (lax.broadcasted_iota(jnp.int32, (tm, tm), 1) < lax.broadcasted_iota(jnp.int32, (tm, tm), 0))
        before = jnp.dot(earlier.astype(BF16), chosen.astype(BF16), preferred_element_type=F32) + run_ref[...]
        r1 = jnp.sum(jnp.where(sel1, before, 0.0), axis=-1, keepdims=True)
        r2 = jnp.sum(jnp.where(sel2, before, 0.0), axis=-1, keepdims=True)
        run_ref[...] += jnp.sum(chosen, axis=0, keepdims=True)
        counts_ref[...] = jnp.broadcast_to(run_ref[...], counts_ref.shape)

        fields = (g1, g2, i1.astype(F32), i2.astype(F32), r1, r2)
        route = jnp.zeros(logits.shape, F32)
        for k, val in enumerate(fields):
            route = jnp.where(lane == k, val, route)
        route_ref[...] = route


def _out_proj(attn, mix, x, w_out, ffn_g, router_w=None):
    n = x.shape[0]
    with_router = router_w is not None
    row_spec = lambda width: pl.BlockSpec((OUT_TM, width), lambda i: (i, 0))
    in_specs = [row_spec(ATTN_WIDTH), row_spec(POOL_WIDTH + CONV_WIDTH), row_spec(D_MODEL),
                pl.BlockSpec((D_MODEL, D_MODEL), lambda i: (0, 0)),
                pl.BlockSpec((1, D_MODEL), lambda i: (0, 0))]
    out_specs = [row_spec(D_MODEL), row_spec(D_MODEL)]
    out_shape = [jax.ShapeDtypeStruct((n, D_MODEL), F32),
                 jax.ShapeDtypeStruct((n, D_MODEL), F32 if with_router else BF16)]
    args = [attn, mix, x, w_out, ffn_g]
    scratch = []
    if with_router:
        in_specs.append(pl.BlockSpec((D_MODEL, LANES), lambda i: (0, 0)))
        out_specs += [row_spec(LANES), pl.BlockSpec((8, LANES), lambda i: (0, 0))]
        out_shape += [jax.ShapeDtypeStruct((n, LANES), F32), jax.ShapeDtypeStruct((8, LANES), F32)]
        args.append(router_w)
        scratch = [pltpu.VMEM((1, LANES), F32)]
    return pl.pallas_call(
        functools.partial(_out_proj_kernel, with_router=with_router),
        grid=(n // OUT_TM,),
        in_specs=in_specs,
        out_specs=out_specs,
        out_shape=out_shape,
        scratch_shapes=scratch,
        compiler_params=_params(("arbitrary",) if with_router else ("parallel",)),
        name="out_proj_router" if with_router else "out_proj",
    )(*args)


def _wait_rows(src_rows, dst_rows, sem):
    pltpu.make_async_copy(src_rows, dst_rows, sem).wait()


SUBLANES = 8
PAD_CHUNK_BITS = (FFN_TM // SUBLANES // 2).bit_length()


def _pad_fill_copies(pad_start_ref, pad_len_ref, nt_ref, zero_ref, xs_ref, sem, act):
    half = zero_ref.shape[0]

    def dead_tile(t, carry):
        for part in range(FFN_TM // half):
            rows = pl.ds(pl.multiple_of(t * FFN_TM + part * half, half), half)
            act(pltpu.make_async_copy(zero_ref, xs_ref.at[rows], sem))
        return carry

    lax.fori_loop(nt_ref[0], xs_ref.shape[0] // FFN_TM, dead_tile, 0)
    for e in range(N_EXPERTS):
        start, length = pad_start_ref[e], pad_len_ref[e]
        head = jnp.minimum((SUBLANES - start % SUBLANES) % SUBLANES, length)
        for i in range(SUBLANES - 1):
            @pl.when(i < head)
            def _():
                act(pltpu.make_async_copy(zero_ref.at[pl.ds(0, 1)], xs_ref.at[pl.ds(start + i, 1)], sem))

        aligned = start + head
        groups = (length - head) // SUBLANES
        for b in reversed(range(PAD_CHUNK_BITS)):
            size = SUBLANES << b

            @pl.when(((groups >> b) & 1) == 1)
            def _():
                covered = ((groups >> (b + 1)) << (b + 1)) * SUBLANES
                rows = pl.ds(pl.multiple_of(aligned + covered, SUBLANES), size)
                act(pltpu.make_async_copy(zero_ref.at[pl.ds(0, size)], xs_ref.at[rows], sem))


def _dispatch_kernel(slot_ref, pad_start_ref, pad_len_ref, nt_ref, h_ref, xs_ref, zero_ref, sem, pad_sem):
    rows = h_ref.shape[0]
    base = pl.program_id(0) * rows * TOP_K
    pad_args = (pad_start_ref, pad_len_ref, nt_ref, zero_ref, xs_ref, pad_sem)

    @pl.when(pl.program_id(0) == 0)
    def _():
        zero_ref[...] = jnp.zeros(zero_ref.shape, F32)
        _pad_fill_copies(*pad_args, lambda copy: copy.start())

    def issue(r, carry):
        for k in range(TOP_K):
            s = slot_ref[base + r * TOP_K + k]
            pltpu.make_async_copy(h_ref.at[pl.ds(r, 1)], xs_ref.at[pl.ds(s, 1)], sem).start()
        return carry

    lax.fori_loop(0, rows, issue, 0, unroll=8)
    for _ in range(TOP_K):
        _wait_rows(h_ref, xs_ref.at[pl.ds(0, rows)], sem)

    @pl.when(pl.program_id(0) == 0)
    def _():
        _pad_fill_copies(*pad_args, lambda copy: copy.wait())


def _dispatch(slot, pad_start, pad_len, n_tiles, h, n_slots):
    n = h.shape[0]
    grid_spec = pltpu.PrefetchScalarGridSpec(
        num_scalar_prefetch=4,
        grid=(n // PERM_TM,),
        in_specs=[pl.BlockSpec((PERM_TM, D_MODEL), lambda i, s, ps, pn, nt: (i, 0))],
        out_specs=pl.BlockSpec(memory_space=pl.ANY),
        scratch_shapes=[pltpu.VMEM((FFN_TM // 2, D_MODEL), F32), pltpu.SemaphoreType.DMA(()),
                        pltpu.SemaphoreType.DMA(())],
    )
    return pl.pallas_call(
        _dispatch_kernel,
        grid_spec=grid_spec,
        out_shape=jax.ShapeDtypeStruct((n_slots, D_MODEL), F32),
        compiler_params=_params(("arbitrary",)),
        name="moe_dispatch",
    )(slot, pad_start, pad_len, n_tiles, h)


def _combine_kernel(slot_ref, x_ref, route_ref, y_ref, o_ref, rows_ref, sem):
    rows = x_ref.shape[0]
    base = pl.program_id(0) * rows * TOP_K

    def issue(r, carry):
        for k in range(TOP_K):
            s = slot_ref[base + r * TOP_K + k]
            pltpu.make_async_copy(y_ref.at[pl.ds(s, 1)], rows_ref.at[k, pl.ds(r, 1)], sem).start()
        return carry

    lax.fori_loop(0, rows, issue, 0, unroll=8)
    for k in range(TOP_K):
        _wait_rows(y_ref.at[pl.ds(0, rows)], rows_ref.at[k], sem)
    route = route_ref[...]
    out = x_ref[...]
    for k in range(TOP_K):
        out = out + route[:, ROUTE_GATE + k:ROUTE_GATE + k + 1] * rows_ref[k]
    o_ref[...] = out


def _combine(slot, x, route, y):
    n = x.shape[0]
    grid_spec = pltpu.PrefetchScalarGridSpec(
        num_scalar_prefetch=1,
        grid=(n // PERM_TM,),
        in_specs=[
            pl.BlockSpec((PERM_TM, D_MODEL), lambda i, s: (i, 0)),
            pl.BlockSpec((PERM_TM, LANES), lambda i, s: (i, 0)),
            pl.BlockSpec(memory_space=pl.ANY),
        ],
        out_specs=pl.BlockSpec((PERM_TM, D_MODEL), lambda i, s: (i, 0)),
        scratch_shapes=[pltpu.VMEM((TOP_K, PERM_TM, D_MODEL), F32), pltpu.SemaphoreType.DMA(())],
    )
    return pl.pallas_call(
        _combine_kernel,
        grid_spec=grid_spec,
        out_shape=jax.ShapeDtypeStruct((n, D_MODEL), F32),
        compiler_params=_params(("arbitrary",)),
        name="moe_combine",
    )(slot, x, route, y)


def _swiglu_step(x, wg_ref, wu_ref, wd_ref):
    g = jnp.dot(x, wg_ref[0].astype(BF16), preferred_element_type=F32)
    u = jnp.dot(x, wu_ref[0].astype(BF16), preferred_element_type=F32)
    a = (g * jax.nn.sigmoid(g) * u).astype(BF16)
    return jnp.dot(a, wd_ref[0].astype(BF16), preferred_element_type=F32)


def _dense_ffn_kernel(x_ref, wg_ref, wu_ref, wd_ref, res_ref, o_ref, acc_ref, *, n_f):
    f = pl.program_id(1)

    @pl.when(f == 0)
    def _():
        acc_ref[...] = jnp.zeros(acc_ref.shape, F32)

    acc_ref[...] += _swiglu_step(x_ref[...], wg_ref, wu_ref, wd_ref)

    @pl.when(f == n_f - 1)
    def _():
        o_ref[...] = res_ref[...] + acc_ref[...]


def _dense_ffn(h, wg, wu, wd, res):
    rows = h.shape[0]
    n_f = wg.shape[-1] // FFN_TF
    once = dict(pipeline_mode=pl.Buffered(1))
    return pl.pallas_call(
        functools.partial(_dense_ffn_kernel, n_f=n_f),
        grid=(rows // FFN_TM, n_f),
        in_specs=[
            pl.BlockSpec((FFN_TM, D_MODEL), lambda t, f: (t, 0)),
            pl.BlockSpec((1, D_MODEL, FFN_TF), lambda t, f: (0, 0, f)),
            pl.BlockSpec((1, D_MODEL, FFN_TF), lambda t, f: (0, 0, f)),
            pl.BlockSpec((1, FFN_TF, D_MODEL), lambda t, f: (0, f, 0)),
            pl.BlockSpec((FFN_TM, D_MODEL), lambda t, f: (t, 0), **once),
        ],
        out_specs=pl.BlockSpec((FFN_TM, D_MODEL), lambda t, f: (t, 0), **once),
        out_shape=jax.ShapeDtypeStruct((rows, D_MODEL), F32),
        scratch_shapes=[pltpu.VMEM((FFN_TM, D_MODEL), F32)],
        compiler_params=_params(("parallel", "arbitrary")),
        name="dense_ffn",
    )(h, wg, wu, wd, res)


def _moe_ffn_kernel(te_ref, nt_ref, x_ref, wg_ref, wu_ref, wd_ref, o_ref, xb_ref):
    t = pl.program_id(0)
    f = pl.program_id(1)
    live = t < nt_ref[0]

    @pl.when(live & (f == 0))
    def _():
        xb_ref[...] = x_ref[...].astype(BF16)
        o_ref[...] = _swiglu_step(xb_ref[...], wg_ref, wu_ref, wd_ref)

    @pl.when(live & (f > 0))
    def _():
        o_ref[...] += _swiglu_step(xb_ref[...], wg_ref, wu_ref, wd_ref)

    @pl.when(jnp.logical_not(live) & (f == 0))
    def _():
        o_ref[...] = jnp.zeros(o_ref.shape, F32)


def _moe_ffn(tile_expert, n_tiles, x_sorted, wg, wu, wd):
    rows = x_sorted.shape[0]
    n_f = wg.shape[-1] // FFN_TF

    def x_tile(t, f, te, nt):
        return (jnp.minimum(t, nt[0] - 1), 0)

    def hidden(t, f, nt):
        return jnp.where(t < nt[0], f, n_f - 1)

    grid_spec = pltpu.PrefetchScalarGridSpec(
        num_scalar_prefetch=2,
        grid=(rows // FFN_TM, n_f),
        in_specs=[
            pl.BlockSpec((FFN_TM, D_MODEL), x_tile),
            pl.BlockSpec((1, D_MODEL, FFN_TF), lambda t, f, te, nt: (te[t], 0, hidden(t, f, nt))),
            pl.BlockSpec((1, D_MODEL, FFN_TF), lambda t, f, te, nt: (te[t], 0, hidden(t, f, nt))),
            pl.BlockSpec((1, FFN_TF, D_MODEL), lambda t, f, te, nt: (te[t], hidden(t, f, nt), 0)),
        ],
        out_specs=pl.BlockSpec((FFN_TM, D_MODEL), lambda t, f, te, nt: (t, 0), pipeline_mode=pl.Buffered(1)),
        scratch_shapes=[pltpu.VMEM((FFN_TM, D_MODEL), BF16)],
    )
    return pl.pallas_call(
        _moe_ffn_kernel,
        grid_spec=grid_spec,
        out_shape=jax.ShapeDtypeStruct((rows, D_MODEL), F32),
        compiler_params=_params(("arbitrary", "arbitrary")),
        name="moe_ffn",
    )(tile_expert, n_tiles, x_sorted, wg, wu, wd)


def _moe_layer(x, h, route, counts, wg, wu, wd):
    n = x.shape[0]
    n_t = (n * TOP_K) // FFN_TM + N_EXPERTS
    experts = jnp.arange(N_EXPERTS, dtype=jnp.int32)
    counts = counts[0, :N_EXPERTS].astype(jnp.int32)
    tiles_per = (counts + FFN_TM - 1) // FFN_TM
    tiles_end = jnp.cumsum(tiles_per)
    first_tile = tiles_end - tiles_per
    idx = route[:, ROUTE_EXPERT:ROUTE_EXPERT + TOP_K].astype(jnp.int32)
    rank = route[:, ROUTE_RANK:ROUTE_RANK + TOP_K].astype(jnp.int32)
    group_start = jnp.sum(jnp.where(idx[:, :, None] == experts, first_tile * FFN_TM, 0), axis=-1)
    slot = (group_start + rank).reshape(-1)

    n_tiles = tiles_end[-1:]
    t_ids = jnp.minimum(jnp.arange(n_t, dtype=jnp.int32), n_tiles[0] - 1)
    tile_expert = jnp.minimum(jnp.sum((t_ids[:, None] >= tiles_end[None, :]).astype(jnp.int32), axis=1),
                              N_EXPERTS - 1)
    pad_start = first_tile * FFN_TM + counts
    pad_len = tiles_per * FFN_TM - counts

    x_sorted = _dispatch(slot, pad_start, pad_len, n_tiles, h, n_t * FFN_TM)
    y = _moe_ffn(tile_expert, n_tiles, x_sorted, wg, wu, wd)
    return _combine(slot, x, route, y)


def kernel(x, positions, attn_norm_g, w_in, q_norm_g, k_norm_g, lambda_vecs, attn_out_norm_g, w_pool, pool_scale,
           conv_w, w_out, ffn_norm_g, dense_w_gate, dense_w_up, dense_w_down, router_w, moe_w_gate, moe_w_up,
           moe_w_down):
    batch, seq, d_model = x.shape
    depth = w_in.shape[0]
    n = batch * seq

    inv_freq = 1.0 / (ROPE_THETA ** (jnp.arange(0, DIFF_HEAD_DIM, 2, dtype=F32) / DIFF_HEAD_DIM))
    ang = positions.astype(F32)[..., None] * inv_freq
    cos = jnp.cos(ang).astype(x.dtype).reshape(n, DIFF_HEAD_DIM // 2)
    sin = jnp.sin(ang).astype(x.dtype).reshape(n, DIFF_HEAD_DIM // 2)
    cos2 = jnp.concatenate([cos, cos], axis=-1)
    sin2 = jnp.concatenate([-sin, sin], axis=-1)

    xf = x.reshape(n, d_model)
    for l in range(depth):
        lam_init = 0.8 - 0.6 * math.exp(-0.3 * l)
        qk_g = jnp.stack([q_norm_g[l], k_norm_g[l]])[:, None, :]
        z = _in_proj(xf, attn_norm_g[l][None, :], w_in[l].astype(BF16), cos2, sin2, qk_g)
        attn = _attention(z, lambda_vecs[l], attn_out_norm_g[l][None, :], lam_init, batch, seq)
        mix = _mixers(z, w_pool[l].astype(BF16), pool_scale[l][None, :], conv_w[l], batch, seq)
        j = l // 2
        if l % 2 == 0:
            xf, h = _out_proj(attn, mix, xf, w_out[l].astype(BF16), ffn_norm_g[l][None, :])
            xf = _dense_ffn(h, dense_w_gate[j][None], dense_w_up[j][None], dense_w_down[j][None], xf)
        else:
            rw = jnp.pad(router_w[j], ((0, 0), (0, LANES - N_EXPERTS)))
            xf, h, route, counts = _out_proj(attn, mix, xf, w_out[l].astype(BF16), ffn_norm_g[l][None, :], rw)
            xf = _moe_layer(xf, h, route, counts, moe_w_gate[j], moe_w_up[j], moe_w_down[j])
    return xf.reshape(batch, seq, d_model)
```

```python
import functools
import math

import jax
import jax.numpy as jnp
from jax import lax
from jax.experimental import pallas as pl
from jax.experimental.pallas import tpu as pltpu

D_MODEL = 2048
DIFF_HEAD_DIM = 128
DIFF_HEADS = 4
HEAD_WIDTH = 2 * DIFF_HEAD_DIM
QK_WIDTH = DIFF_HEADS * HEAD_WIDTH
ATTN_WIDTH = QK_WIDTH
POOL_WINDOWS = (2, 4, 8, 16)
POOL_GROUP = 128
POOL_WIDTH = 512
CONV_WIDTH = 512
CONV_TAPS = 3
IN_WIDTH = 5120
ROPE_THETA = 10000.0
N_EXPERTS = 8
TOP_K = 2
EPS = 1e-6
NEG_INF = -1e30
LANES = 128

BF16 = jnp.bfloat16
F32 = jnp.float32

VMEM_LIMIT_BYTES = 56 * 1024 * 1024

IN_TM = 1024
IN_TN = 1024
ATTN_TQ = 512
OUT_TM = 512
FFN_TM = 1024
FFN_TF = 256
FFN_SUB = 256
PERM_TM = 512


def _params(semantics):
    return pltpu.CompilerParams(dimension_semantics=semantics, vmem_limit_bytes=VMEM_LIMIT_BYTES)


def _in_proj_kernel(x_ref, g_ref, w_ref, cos_ref, sin_ref, qkg_ref, zqk_ref, zrest_ref, h_ref, rawq_ref, rawk_ref):
    j = pl.program_id(1)

    def project():
        return jnp.dot(h_ref[...], w_ref[...], preferred_element_type=F32)

    def head_norm_rotary(raw_ref, which):
        gain = qkg_ref[which]
        cos = cos_ref[...]
        sin = sin_ref[...]
        for c in range(IN_TN // DIFF_HEAD_DIM):
            cols = slice(c * DIFF_HEAD_DIM, (c + 1) * DIFF_HEAD_DIM)
            zc = raw_ref[:, cols]
            ms = jnp.mean(zc * zc, axis=-1, keepdims=True)
            zn = zc * lax.rsqrt(ms + EPS) * gain
            rot = zn * cos + pltpu.roll(zn, DIFF_HEAD_DIM // 2, axis=1) * sin
            if which == 0:
                rot = rot * DIFF_HEAD_DIM ** -0.5
            zqk_ref[:, cols] = rot.astype(BF16)

    @pl.when(j == 0)
    def _():
        x = x_ref[...]
        ms = jnp.mean(x * x, axis=-1, keepdims=True)
        h_ref[...] = (x * lax.rsqrt(ms + EPS) * g_ref[...]).astype(BF16)
        rawq_ref[...] = project()

    @pl.when(j == 1)
    def _():
        rawk_ref[...] = project()
        head_norm_rotary(rawq_ref, 0)

    @pl.when(j == 2)
    def _():
        zrest_ref[...] = project().astype(BF16)
        head_norm_rotary(rawk_ref, 1)

    @pl.when(j > 2)
    def _():
        zrest_ref[...] = project().astype(BF16)


def _in_proj(x, norm_g, w_in, cos2, sin2, qk_g):
    n = x.shape[0]
    n_qk = 2 * QK_WIDTH // IN_TN
    grid = (n // IN_TM, IN_WIDTH // IN_TN)
    return pl.pallas_call(
        _in_proj_kernel,
        grid=grid,
        in_specs=[
            pl.BlockSpec((IN_TM, D_MODEL), lambda i, j: (i, 0)),
            pl.BlockSpec((1, D_MODEL), lambda i, j: (0, 0)),
            pl.BlockSpec((D_MODEL, IN_TN), lambda i, j: (0, j)),
            pl.BlockSpec((IN_TM, DIFF_HEAD_DIM), lambda i, j: (i, 0)),
            pl.BlockSpec((IN_TM, DIFF_HEAD_DIM), lambda i, j: (i, 0)),
            pl.BlockSpec((2, 1, DIFF_HEAD_DIM), lambda i, j: (0, 0, 0)),
        ],
        out_specs=[
            pl.BlockSpec((IN_TM, IN_TN), lambda i, j: (i, jnp.clip(j - 1, 0, n_qk - 1))),
            pl.BlockSpec((IN_TM, IN_TN), lambda i, j: (i, jnp.maximum(j - n_qk, 0))),
        ],
        out_shape=[jax.ShapeDtypeStruct((n, 2 * QK_WIDTH), BF16),
                   jax.ShapeDtypeStruct((n, IN_WIDTH - 2 * QK_WIDTH), BF16)],
        scratch_shapes=[pltpu.VMEM((IN_TM, D_MODEL), BF16), pltpu.VMEM((IN_TM, IN_TN), F32),
                        pltpu.VMEM((IN_TM, IN_TN), F32)],
        compiler_params=_params(("parallel", "arbitrary")),
        name="in_proj",
    )(x, norm_g, w_in, cos2, sin2, qk_g)


def _attn_kernel(q_ref, k_ref, v_ref, lv_ref, subg_ref, o_ref, *, lam_init):
    t, d = ATTN_TQ, DIFF_HEAD_DIM
    seq = q_ref.shape[0]
    lv = lv_ref[...]
    lam = (jnp.exp(jnp.sum(lv[0:1] * lv[1:2], axis=-1, keepdims=True))
           - jnp.exp(jnp.sum(lv[2:3] * lv[3:4], axis=-1, keepdims=True)) + lam_init)
    causal = (lax.broadcasted_iota(jnp.int32, (t, t), 1) <= lax.broadcasted_iota(jnp.int32, (t, t), 0))
    nt_dims = (((1,), (1,)), ((), ()))

    for i in range(seq // t):
        lo = i * t
        a_diag, a_past = None, None
        for mi in range(2):
            cols = slice(mi * d, (mi + 1) * d)
            qm = q_ref[lo:lo + t, cols]
            s_diag = lax.dot_general(qm, k_ref[lo:lo + t, cols], nt_dims, preferred_element_type=F32)
            s_diag = jnp.where(causal, s_diag, NEG_INF)
            mx = jnp.max(s_diag, axis=-1, keepdims=True)
            if i > 0:
                s_past = lax.dot_general(qm, k_ref[0:lo, cols], nt_dims, preferred_element_type=F32)
                mx = jnp.maximum(mx, jnp.max(s_past, axis=-1, keepdims=True))
            p_diag = jnp.exp(s_diag - mx)
            denom = jnp.sum(p_diag, axis=-1, keepdims=True)
            if i > 0:
                p_past = jnp.exp(s_past - mx)
                denom = denom + jnp.sum(p_past, axis=-1, keepdims=True)
            weight = 1.0 / denom if mi == 0 else lam / denom
            if mi == 0:
                a_diag = p_diag * weight
                a_past = p_past * weight if i > 0 else None
            else:
                a_diag = a_diag - p_diag * weight
                a_past = a_past - p_past * weight if i > 0 else None
        o = jnp.dot(a_diag.astype(BF16), v_ref[lo:lo + t, :], preferred_element_type=F32)
        if i > 0:
            o = o + jnp.dot(a_past.astype(BF16), v_ref[0:lo, :], preferred_element_type=F32)
        ms = jnp.mean(o * o, axis=-1, keepdims=True)
        o_ref[lo:lo + t, :] = ((o * lax.rsqrt(ms + EPS)) * subg_ref[...] * (1.0 - lam_init)).astype(BF16)


def _attention(z_qk, z_rest, lam_vecs, sub_g, lam_init, batch, seq):
    k_col0 = QK_WIDTH // HEAD_WIDTH
    return pl.pallas_call(
        functools.partial(_attn_kernel, lam_init=lam_init),
        grid=(batch, DIFF_HEADS),
        in_specs=[
            pl.BlockSpec((seq, HEAD_WIDTH), lambda b, h: (b, h)),
            pl.BlockSpec((seq, HEAD_WIDTH), lambda b, h: (b, k_col0 + h)),
            pl.BlockSpec((seq, HEAD_WIDTH), lambda b, h: (b, h)),
            pl.BlockSpec((4, DIFF_HEAD_DIM), lambda b, h: (0, 0)),
            pl.BlockSpec((1, HEAD_WIDTH), lambda b, h: (0, 0)),
        ],
        out_specs=pl.BlockSpec((seq, HEAD_WIDTH), lambda b, h: (b, h)),
        out_shape=jax.ShapeDtypeStruct((batch * seq, ATTN_WIDTH), BF16),
        compiler_params=_params(("parallel", "parallel")),
        name="diff_attention",
    )(z_qk, z_qk, z_rest, lam_vecs, sub_g)


def _mixers_kernel(up_ref, gb_ref, gc_ref, uc_ref, wp_ref, ps_ref, cw_ref, o_ref):
    seq = up_ref.shape[0]
    row = lax.broadcasted_iota(jnp.int32, (seq, LANES), 0)

    def delayed(x, k):
        return jnp.where(row >= k, pltpu.roll(x, k, axis=0), 0.0)

    for gi, w in enumerate(POOL_WINDOWS):
        cols = slice(gi * POOL_GROUP, (gi + 1) * POOL_GROUP)
        g = up_ref[:, cols].astype(F32)
        s, span = g, 1
        while span < w:
            s = s + delayed(s, span)
            span *= 2
        count = jnp.minimum(row + 1, w).astype(F32)
        pooled = s / count - g
        y = jnp.dot(pooled.astype(BF16), wp_ref[gi], preferred_element_type=F32) * ps_ref[:, cols]
        o_ref[:, cols] = y.astype(BF16)

    for c in range(CONV_WIDTH // LANES):
        cols = slice(c * LANES, (c + 1) * LANES)
        u = gc_ref[:, cols].astype(F32) * uc_ref[:, cols].astype(F32)
        y = (cw_ref[0:1, cols] * delayed(u, 2) + cw_ref[1:2, cols] * delayed(u, 1)) + cw_ref[2:3, cols] * u
        out = gb_ref[:, cols].astype(F32) * y
        o_ref[:, POOL_WIDTH + c * LANES:POOL_WIDTH + (c + 1) * LANES] = out.astype(BF16)


def _mixers(z, w_pool, pool_scale, conv_w, batch, seq):
    col0 = ATTN_WIDTH // POOL_WIDTH
    zspec = lambda off: pl.BlockSpec((seq, POOL_WIDTH), lambda b: (b, col0 + off))
    return pl.pallas_call(
        _mixers_kernel,
        grid=(batch,),
        in_specs=[
            zspec(0), zspec(1), zspec(2), zspec(3),
            pl.BlockSpec((len(POOL_WINDOWS), POOL_GROUP, POOL_GROUP), lambda b: (0, 0, 0)),
            pl.BlockSpec((1, POOL_WIDTH), lambda b: (0, 0)),
            pl.BlockSpec((CONV_TAPS, CONV_WIDTH), lambda b: (0, 0)),
        ],
        out_specs=pl.BlockSpec((seq, POOL_WIDTH + CONV_WIDTH), lambda b: (b, 0)),
        out_shape=jax.ShapeDtypeStruct((batch * seq, POOL_WIDTH + CONV_WIDTH), BF16),
        compiler_params=_params(("parallel",)),
        name="pool_conv_mixers",
    )(z, z, z, z, w_pool, pool_scale, conv_w)


ROUTE_GATE, ROUTE_EXPERT, ROUTE_RANK = 0, 2, 4


def _out_proj_kernel(*refs, with_router):
    if with_router:
        attn_ref, mix_ref, x_ref, wo_ref, g_ref, rw_ref, xo_ref, h_ref, route_ref, counts_ref, run_ref, r2_ref = refs
    else:
        attn_ref, mix_ref, x_ref, wo_ref, g_ref, xo_ref, h_ref = refs
    y = jnp.dot(attn_ref[...], wo_ref[:ATTN_WIDTH, :], preferred_element_type=F32)
    y = y + jnp.dot(mix_ref[...], wo_ref[ATTN_WIDTH:, :], preferred_element_type=F32)
    xn = x_ref[...] + y
    xo_ref[...] = xn
    ms = jnp.mean(xn * xn, axis=-1, keepdims=True)
    hf = xn * lax.rsqrt(ms + EPS) * g_ref[...]
    h_ref[...] = hf.astype(h_ref.dtype)
    if with_router:
        @pl.when(pl.program_id(0) == 0)
        def _():
            rw = rw_ref[...]
            r_hi = rw.astype(BF16)
            r2_ref[:, :LANES] = r_hi
            r2_ref[:, LANES:] = (rw - r_hi.astype(F32)).astype(BF16)
            run_ref[...] = jnp.zeros(run_ref.shape, F32)

        h_hi = hf.astype(BF16)
        h_lo = (hf - h_hi.astype(F32)).astype(BF16)
        hi_terms = jnp.dot(h_hi, r2_ref[...], preferred_element_type=F32)
        logits = (hi_terms[:, :LANES] + hi_terms[:, LANES:]
                  + jnp.dot(h_lo, r2_ref[:, :LANES], preferred_element_type=F32))
        tm = logits.shape[0]
        lane = lax.broadcasted_iota(jnp.int32, logits.shape, 1)
        logits = jnp.where(lane < N_EXPERTS, logits, -jnp.inf)
        v1 = jnp.max(logits, axis=-1, keepdims=True)
        i1 = jnp.min(jnp.where(logits == v1, lane, LANES), axis=-1, keepdims=True)
        rest = jnp.where(lane == i1, -jnp.inf, logits)
        v2 = jnp.max(rest, axis=-1, keepdims=True)
        i2 = jnp.min(jnp.where(rest == v2, lane, LANES), axis=-1, keepdims=True)
        e2 = jnp.exp(v2 - v1)
        g1 = 1.0 / (1.0 + e2)
        g2 = e2 / (1.0 + e2)

        sel1 = lane == i1
        sel2 = lane == i2
        chosen = jnp.where(sel1 | sel2, 1.0, 0.0)
        earlier = (lax.broadcasted_iota(jnp.int32, (tm, tm), 1) < lax.broadcasted_iota(jnp.int32, (tm, tm), 0))
        before = jnp.dot(earlier.astype(BF16), chosen.astype(BF16), preferred_element_type=F32) + run_ref[...]
        r1 = jnp.sum(jnp.where(sel1, before, 0.0), axis=-1, keepdims=True)
        r2 = jnp.sum(jnp.where(sel2, before, 0.0), axis=-1, keepdims=True)
        run_ref[...] += jnp.sum(chosen, axis=0, keepdims=True)
        counts_ref[...] = jnp.broadcast_to(run_ref[...], counts_ref.shape)

        fields = (g1, g2, i1.astype(F32), i2.astype(F32), r1, r2)
        route = jnp.zeros(logits.shape, F32)
        for k, val in enumerate(fields):
            route = jnp.where(lane == k, val, route)
        route_ref[...] = route


def _out_proj(attn, mix, x, w_out, ffn_g, router_w=None):
    n = x.shape[0]
    with_router = router_w is not None
    row_spec = lambda width: pl.BlockSpec((OUT_TM, width), lambda i: (i, 0))
    in_specs = [row_spec(ATTN_WIDTH), row_spec(POOL_WIDTH + CONV_WIDTH), row_spec(D_MODEL),
                pl.BlockSpec((D_MODEL, D_MODEL), lambda i: (0, 0)),
                pl.BlockSpec((1, D_MODEL), lambda i: (0, 0))]
    out_specs = [row_spec(D_MODEL), row_spec(D_MODEL)]
    out_shape = [jax.ShapeDtypeStruct((n, D_MODEL), F32),
                 jax.ShapeDtypeStruct((n, D_MODEL), F32 if with_router else BF16)]
    args = [attn, mix, x, w_out, ffn_g]
    scratch = []
    if with_router:
        in_specs.append(pl.BlockSpec((D_MODEL, LANES), lambda i: (0, 0)))
        out_specs += [row_spec(LANES), pl.BlockSpec((8, LANES), lambda i: (0, 0))]
        out_shape += [jax.ShapeDtypeStruct((n, LANES), F32), jax.ShapeDtypeStruct((8, LANES), F32)]
        args.append(router_w)
        scratch = [pltpu.VMEM((1, LANES), F32), pltpu.VMEM((D_MODEL, 2 * LANES), BF16)]
    return pl.pallas_call(
        functools.partial(_out_proj_kernel, with_router=with_router),
        grid=(n // OUT_TM,),
        in_specs=in_specs,
        out_specs=out_specs,
        out_shape=out_shape,
        scratch_shapes=scratch,
        compiler_params=_params(("arbitrary",) if with_router else ("parallel",)),
        name="out_proj_router" if with_router else "out_proj",
    )(*args)


def _wait_rows(src_rows, dst_rows, sem):
    pltpu.make_async_copy(src_rows, dst_rows, sem).wait()


SUBLANES = 8
PAD_CHUNK_BITS = (FFN_TM // SUBLANES // 2).bit_length()


def _pad_fill_copies(pad_start_ref, pad_len_ref, nt_ref, zero_ref, xs_ref, sem, act):
    half = zero_ref.shape[0]

    def dead_tile(t, carry):
        for part in range(FFN_TM // half):
            rows = pl.ds(pl.multiple_of(t * FFN_TM + part * half, half), half)
            act(pltpu.make_async_copy(zero_ref, xs_ref.at[rows], sem))
        return carry

    lax.fori_loop(nt_ref[0], xs_ref.shape[0] // FFN_TM, dead_tile, 0)
    for e in range(N_EXPERTS):
        start, length = pad_start_ref[e], pad_len_ref[e]
        head = jnp.minimum((SUBLANES - start % SUBLANES) % SUBLANES, length)
        for i in range(SUBLANES - 1):
            @pl.when(i < head)
            def _():
                act(pltpu.make_async_copy(zero_ref.at[pl.ds(0, 1)], xs_ref.at[pl.ds(start + i, 1)], sem))

        aligned = start + head
        groups = (length - head) // SUBLANES
        for b in reversed(range(PAD_CHUNK_BITS)):
            size = SUBLANES << b

            @pl.when(((groups >> b) & 1) == 1)
            def _():
                covered = ((groups >> (b + 1)) << (b + 1)) * SUBLANES
                rows = pl.ds(pl.multiple_of(aligned + covered, SUBLANES), size)
                act(pltpu.make_async_copy(zero_ref.at[pl.ds(0, size)], xs_ref.at[rows], sem))


def _dispatch_kernel(slot_ref, pad_start_ref, pad_len_ref, nt_ref, h_ref, xs_ref, zero_ref, sem, pad_sem):
    rows = h_ref.shape[0]
    base = pl.program_id(0) * rows * TOP_K
    pad_args = (pad_start_ref, pad_len_ref, nt_ref, zero_ref, xs_ref, pad_sem)

    @pl.when(pl.program_id(0) == 0)
    def _():
        zero_ref[...] = jnp.zeros(zero_ref.shape, F32)
        _pad_fill_copies(*pad_args, lambda copy: copy.start())

    def issue(r, carry):
        for k in range(TOP_K):
            s = slot_ref[base + r * TOP_K + k]
            pltpu.make_async_copy(h_ref.at[pl.ds(r, 1)], xs_ref.at[pl.ds(s, 1)], sem).start()
        return carry

    lax.fori_loop(0, rows, issue, 0, unroll=8)
    for _ in range(TOP_K):
        _wait_rows(h_ref, xs_ref.at[pl.ds(0, rows)], sem)

    @pl.when(pl.program_id(0) == 0)
    def _():
        _pad_fill_copies(*pad_args, lambda copy: copy.wait())


def _dispatch(slot, pad_start, pad_len, n_tiles, h, n_slots):
    n = h.shape[0]
    grid_spec = pltpu.PrefetchScalarGridSpec(
        num_scalar_prefetch=4,
        grid=(n // PERM_TM,),
        in_specs=[pl.BlockSpec((PERM_TM, D_MODEL), lambda i, s, ps, pn, nt: (i, 0))],
        out_specs=pl.BlockSpec(memory_space=pl.ANY),
        scratch_shapes=[pltpu.VMEM((FFN_TM // 2, D_MODEL), F32), pltpu.SemaphoreType.DMA(()),
                        pltpu.SemaphoreType.DMA(())],
    )
    return pl.pallas_call(
        _dispatch_kernel,
        grid_spec=grid_spec,
        out_shape=jax.ShapeDtypeStruct((n_slots, D_MODEL), F32),
        compiler_params=_params(("arbitrary",)),
        name="moe_dispatch",
    )(slot, pad_start, pad_len, n_tiles, h)


def _combine_kernel(slot_ref, x_ref, route_ref, y_ref, o_ref, rows_ref, sem):
    rows = x_ref.shape[0]
    base = pl.program_id(0) * rows * TOP_K

    def issue(r, carry):
        for k in range(TOP_K):
            s = slot_ref[base + r * TOP_K + k]
            pltpu.make_async_copy(y_ref.at[pl.ds(s, 1)], rows_ref.at[k, pl.ds(r, 1)], sem).start()
        return carry

    lax.fori_loop(0, rows, issue, 0, unroll=8)
    for k in range(TOP_K):
        _wait_rows(y_ref.at[pl.ds(0, rows)], rows_ref.at[k], sem)
    route = route_ref[...]
    out = x_ref[...]
    for k in range(TOP_K):
        out = out + route[:, ROUTE_GATE + k:ROUTE_GATE + k + 1] * rows_ref[k]
    o_ref[...] = out


def _combine(slot, x, route, y):
    n = x.shape[0]
    grid_spec = pltpu.PrefetchScalarGridSpec(
        num_scalar_prefetch=1,
        grid=(n // PERM_TM,),
        in_specs=[
            pl.BlockSpec((PERM_TM, D_MODEL), lambda i, s: (i, 0)),
            pl.BlockSpec((PERM_TM, LANES), lambda i, s: (i, 0)),
            pl.BlockSpec(memory_space=pl.ANY),
        ],
        out_specs=pl.BlockSpec((PERM_TM, D_MODEL), lambda i, s: (i, 0)),
        scratch_shapes=[pltpu.VMEM((TOP_K, PERM_TM, D_MODEL), F32), pltpu.SemaphoreType.DMA(())],
    )
    return pl.pallas_call(
        _combine_kernel,
        grid_spec=grid_spec,
        out_shape=jax.ShapeDtypeStruct((n, D_MODEL), F32),
        compiler_params=_params(("arbitrary",)),
        name="moe_combine",
    )(slot, x, route, y)


def _swiglu_rows(x, wg, wu, wd):
    g = jnp.dot(x, wg, preferred_element_type=F32)
    u = jnp.dot(x, wu, preferred_element_type=F32)
    a = (g * jax.nn.sigmoid(g) * u).astype(BF16)
    return jnp.dot(a, wd, preferred_element_type=F32)


def _swiglu_step(x, wg_ref, wu_ref, wd_ref):
    return _swiglu_rows(x, wg_ref[0].astype(BF16), wu_ref[0].astype(BF16), wd_ref[0].astype(BF16))


def _dense_ffn_kernel(x_ref, wg_ref, wu_ref, wd_ref, res_ref, o_ref, acc_ref, *, n_f):
    f = pl.program_id(1)

    @pl.when(f == 0)
    def _():
        acc_ref[...] = jnp.zeros(acc_ref.shape, F32)

    acc_ref[...] += _swiglu_step(x_ref[...], wg_ref, wu_ref, wd_ref)

    @pl.when(f == n_f - 1)
    def _():
        o_ref[...] = res_ref[...] + acc_ref[...]


def _dense_ffn(h, wg, wu, wd, res):
    rows = h.shape[0]
    n_f = wg.shape[-1] // FFN_TF
    once = dict(pipeline_mode=pl.Buffered(1))
    return pl.pallas_call(
        functools.partial(_dense_ffn_kernel, n_f=n_f),
        grid=(rows // FFN_TM, n_f),
        in_specs=[
            pl.BlockSpec((FFN_TM, D_MODEL), lambda t, f: (t, 0)),
            pl.BlockSpec((1, D_MODEL, FFN_TF), lambda t, f: (0, 0, f)),
            pl.BlockSpec((1, D_MODEL, FFN_TF), lambda t, f: (0, 0, f)),
            pl.BlockSpec((1, FFN_TF, D_MODEL), lambda t, f: (0, f, 0)),
            pl.BlockSpec((FFN_TM, D_MODEL), lambda t, f: (t, 0), **once),
        ],
        out_specs=pl.BlockSpec((FFN_TM, D_MODEL), lambda t, f: (t, 0), **once),
        out_shape=jax.ShapeDtypeStruct((rows, D_MODEL), F32),
        scratch_shapes=[pltpu.VMEM((FFN_TM, D_MODEL), F32)],
        compiler_params=_params(("parallel", "arbitrary")),
        name="dense_ffn",
    )(h, wg, wu, wd, res)


def _moe_ffn_kernel(te_ref, nt_ref, tv_ref, x_ref, wg_ref, wu_ref, wd_ref, o_ref, xb_ref):
    t = pl.program_id(0)
    f = pl.program_id(1)
    live = t < nt_ref[0]
    valid = tv_ref[t]

    @pl.when(f == 0)
    def _():
        o_ref[...] = jnp.zeros(o_ref.shape, F32)

        @pl.when(live)
        def _():
            xb_ref[...] = x_ref[...].astype(BF16)

    @pl.when(live & (valid == FFN_TM))
    def _():
        o_ref[...] += _swiglu_step(xb_ref[...], wg_ref, wu_ref, wd_ref)

    @pl.when(live & (valid < FFN_TM))
    def _():
        weights = (wg_ref[0].astype(BF16), wu_ref[0].astype(BF16), wd_ref[0].astype(BF16))
        for c in range(FFN_TM // FFN_SUB):
            rows = slice(c * FFN_SUB, (c + 1) * FFN_SUB)

            @pl.when(c * FFN_SUB < valid)
            def _():
                o_ref[rows, :] += _swiglu_rows(xb_ref[rows, :], *weights)


def _moe_ffn(tile_expert, n_tiles, tile_valid, x_sorted, wg, wu, wd):
    rows = x_sorted.shape[0]
    n_f = wg.shape[-1] // FFN_TF

    def x_tile(t, f, te, nt, tv):
        return (jnp.minimum(t, nt[0] - 1), 0)

    def hidden(t, f, nt):
        return jnp.where(t < nt[0], f, n_f - 1)

    grid_spec = pltpu.PrefetchScalarGridSpec(
        num_scalar_prefetch=3,
        grid=(rows // FFN_TM, n_f),
        in_specs=[
            pl.BlockSpec((FFN_TM, D_MODEL), x_tile),
            pl.BlockSpec((1, D_MODEL, FFN_TF), lambda t, f, te, nt, tv: (te[t], 0, hidden(t, f, nt))),
            pl.BlockSpec((1, D_MODEL, FFN_TF), lambda t, f, te, nt, tv: (te[t], 0, hidden(t, f, nt))),
            pl.BlockSpec((1, FFN_TF, D_MODEL), lambda t, f, te, nt, tv: (te[t], hidden(t, f, nt), 0)),
        ],
        out_specs=pl.BlockSpec((FFN_TM, D_MODEL), lambda t, f, te, nt, tv: (t, 0), pipeline_mode=pl.Buffered(1)),
        scratch_shapes=[pltpu.VMEM((FFN_TM, D_MODEL), BF16)],
    )
    return pl.pallas_call(
        _moe_ffn_kernel,
        grid_spec=grid_spec,
        out_shape=jax.ShapeDtypeStruct((rows, D_MODEL), F32),
        compiler_params=_params(("arbitrary", "arbitrary")),
        name="moe_ffn",
    )(tile_expert, n_tiles, tile_valid, x_sorted, wg, wu, wd)


def _moe_layer(x, h, route, counts, wg, wu, wd):
    n = x.shape[0]
    n_t = (n * TOP_K) // FFN_TM + N_EXPERTS
    experts = jnp.arange(N_EXPERTS, dtype=jnp.int32)
    counts = counts[0, :N_EXPERTS].astype(jnp.int32)
    tiles_per = (counts + FFN_TM - 1) // FFN_TM
    tiles_end = jnp.cumsum(tiles_per)
    first_tile = tiles_end - tiles_per
    idx = route[:, ROUTE_EXPERT:ROUTE_EXPERT + TOP_K].astype(jnp.int32)
    rank = route[:, ROUTE_RANK:ROUTE_RANK + TOP_K].astype(jnp.int32)
    group_start = jnp.sum(jnp.where(idx[:, :, None] == experts, first_tile * FFN_TM, 0), axis=-1)
    slot = (group_start + rank).reshape(-1)

    n_tiles = tiles_end[-1:]
    t_ids = jnp.minimum(jnp.arange(n_t, dtype=jnp.int32), n_tiles[0] - 1)
    tile_expert = jnp.minimum(jnp.sum((t_ids[:, None] >= tiles_end[None, :]).astype(jnp.int32), axis=1),
                              N_EXPERTS - 1)
    of_tile = tile_expert[:, None] == experts[None, :]
    tile_valid = jnp.clip(jnp.sum(jnp.where(of_tile, counts - (t_ids[:, None] - first_tile) * FFN_TM, 0), axis=1),
                          0, FFN_TM)
    pad_start = first_tile * FFN_TM + counts
    pad_len = tiles_per * FFN_TM - counts

    x_sorted = _dispatch(slot, pad_start, pad_len, n_tiles, h, n_t * FFN_TM)
    y = _moe_ffn(tile_expert, n_tiles, tile_valid, x_sorted, wg, wu, wd)
    return _combine(slot, x, route, y)


def kernel(x, positions, attn_norm_g, w_in, q_norm_g, k_norm_g, lambda_vecs, attn_out_norm_g, w_pool, pool_scale,
           conv_w, w_out, ffn_norm_g, dense_w_gate, dense_w_up, dense_w_down, router_w, moe_w_gate, moe_w_up,
           moe_w_down):
    batch, seq, d_model = x.shape
    depth = w_in.shape[0]
    n = batch * seq

    inv_freq = 1.0 / (ROPE_THETA ** (jnp.arange(0, DIFF_HEAD_DIM, 2, dtype=F32) / DIFF_HEAD_DIM))
    ang = positions.astype(F32)[..., None] * inv_freq
    cos = jnp.cos(ang).astype(x.dtype).reshape(n, DIFF_HEAD_DIM // 2)
    sin = jnp.sin(ang).astype(x.dtype).reshape(n, DIFF_HEAD_DIM // 2)
    cos2 = jnp.concatenate([cos, cos], axis=-1)
    sin2 = jnp.concatenate([-sin, sin], axis=-1)

    xf = x.reshape(n, d_model)
    for l in range(depth):
        lam_init = 0.8 - 0.6 * math.exp(-0.3 * l)
        qk_g = jnp.stack([q_norm_g[l], k_norm_g[l]])[:, None, :]
        z_qk, z_rest = _in_proj(xf, attn_norm_g[l][None, :], w_in[l].astype(BF16), cos2, sin2, qk_g)
        attn = _attention(z_qk, z_rest, lambda_vecs[l], attn_out_norm_g[l][None, :], lam_init, batch, seq)
        mix = _mixers(z_rest, w_pool[l].astype(BF16), pool_scale[l][None, :], conv_w[l], batch, seq)
        j = l // 2
        if l % 2 == 0:
            xf, h = _out_proj(attn, mix, xf, w_out[l].astype(BF16), ffn_norm_g[l][None, :])
            xf = _dense_ffn(h, dense_w_gate[j][None], dense_w_up[j][None], dense_w_down[j][None], xf)
        else:
            rw = jnp.pad(router_w[j], ((0, 0), (0, LANES - N_EXPERTS)))
            xf, h, route, counts = _out_proj(attn, mix, xf, w_out[l].astype(BF16), ffn_norm_g[l][None, :], rw)
            xf = _moe_layer(xf, h, route, counts, moe_w_gate[j], moe_w_up[j], moe_w_down[j])
    return xf.reshape(batch, seq, d_model)
```

```python
import functools
import math

import jax
import jax.numpy as jnp
from jax import lax
from jax.experimental import pallas as pl
from jax.experimental.pallas import tpu as pltpu

D_MODEL = 2048
DIFF_HEAD_DIM = 128
DIFF_HEADS = 4
HEAD_WIDTH = 2 * DIFF_HEAD_DIM
QK_WIDTH = DIFF_HEADS * HEAD_WIDTH
ATTN_WIDTH = QK_WIDTH
POOL_WINDOWS = (2, 4, 8, 16)
POOL_GROUP = 128
POOL_WIDTH = 512
CONV_WIDTH = 512
CONV_TAPS = 3
IN_WIDTH = 5120
ROPE_THETA = 10000.0
N_EXPERTS = 8
TOP_K = 2
EPS = 1e-6
NEG_INF = -1e30
LANES = 128

BF16 = jnp.bfloat16
F32 = jnp.float32

VMEM_LIMIT_BYTES = 56 * 1024 * 1024

IN_TM = 1024
IN_TN = 1024
ATTN_TQ = 512
OUT_TM = 512
FFN_TM = 1024
FFN_TF = 256
PERM_TM = 512


def _params(semantics):
    return pltpu.CompilerParams(dimension_semantics=semantics, vmem_limit_bytes=VMEM_LIMIT_BYTES)


def _in_proj_kernel(x_ref, g_ref, w_ref, cos_ref, sin_ref, qkg_ref, zqk_ref, zrest_ref, h_ref, rawq_ref, rawk_ref):
    j = pl.program_id(1)

    def project():
        return jnp.dot(h_ref[...], w_ref[...], preferred_element_type=F32)

    def head_norm_rotary(raw_ref, which):
        gain = qkg_ref[which]
        cos = cos_ref[...]
        sin = sin_ref[...]
        for c in range(IN_TN // DIFF_HEAD_DIM):
            cols = slice(c * DIFF_HEAD_DIM, (c + 1) * DIFF_HEAD_DIM)
            zc = raw_ref[:, cols]
            ms = jnp.mean(zc * zc, axis=-1, keepdims=True)
            zn = zc * lax.rsqrt(ms + EPS) * gain
            rot = zn * cos + pltpu.roll(zn, DIFF_HEAD_DIM // 2, axis=1) * sin
            if which == 0:
                rot = rot * DIFF_HEAD_DIM ** -0.5
            zqk_ref[:, cols] = rot.astype(BF16)

    @pl.when(j == 0)
    def _():
        x = x_ref[...]
        ms = jnp.mean(x * x, axis=-1, keepdims=True)
        h_ref[...] = (x * lax.rsqrt(ms + EPS) * g_ref[...]).astype(BF16)
        rawq_ref[...] = project()

    @pl.when(j == 1)
    def _():
        rawk_ref[...] = project()
        head_norm_rotary(rawq_ref, 0)

    @pl.when(j == 2)
    def _():
        zrest_ref[...] = project().astype(BF16)
        head_norm_rotary(rawk_ref, 1)

    @pl.when(j > 2)
    def _():
        zrest_ref[...] = project().astype(BF16)


def _in_proj(x, norm_g, w_in, cos2, sin2, qk_g):
    n = x.shape[0]
    n_qk = 2 * QK_WIDTH // IN_TN
    grid = (n // IN_TM, IN_WIDTH // IN_TN)
    return pl.pallas_call(
        _in_proj_kernel,
        grid=grid,
        in_specs=[
            pl.BlockSpec((IN_TM, D_MODEL), lambda i, j: (i, 0)),
            pl.BlockSpec((1, D_MODEL), lambda i, j: (0, 0)),
            pl.BlockSpec((D_MODEL, IN_TN), lambda i, j: (0, j)),
            pl.BlockSpec((IN_TM, DIFF_HEAD_DIM), lambda i, j: (i, 0)),
            pl.BlockSpec((IN_TM, DIFF_HEAD_DIM), lambda i, j: (i, 0)),
            pl.BlockSpec((2, 1, DIFF_HEAD_DIM), lambda i, j: (0, 0, 0)),
        ],
        out_specs=[
            pl.BlockSpec((IN_TM, IN_TN), lambda i, j: (i, jnp.clip(j - 1, 0, n_qk - 1))),
            pl.BlockSpec((IN_TM, IN_TN), lambda i, j: (i, jnp.maximum(j - n_qk, 0))),
        ],
        out_shape=[jax.ShapeDtypeStruct((n, 2 * QK_WIDTH), BF16),
                   jax.ShapeDtypeStruct((n, IN_WIDTH - 2 * QK_WIDTH), BF16)],
        scratch_shapes=[pltpu.VMEM((IN_TM, D_MODEL), BF16), pltpu.VMEM((IN_TM, IN_TN), F32),
                        pltpu.VMEM((IN_TM, IN_TN), F32)],
        compiler_params=_params(("parallel", "arbitrary")),
        name="in_proj",
    )(x, norm_g, w_in, cos2, sin2, qk_g)


def _attn_kernel(q_ref, k_ref, v_ref, lv_ref, subg_ref, o_ref, *, lam_init):
    t, d = ATTN_TQ, DIFF_HEAD_DIM
    seq = q_ref.shape[0]
    lv = lv_ref[...]
    lam = (jnp.exp(jnp.sum(lv[0:1] * lv[1:2], axis=-1, keepdims=True))
           - jnp.exp(jnp.sum(lv[2:3] * lv[3:4], axis=-1, keepdims=True)) + lam_init)
    causal = (lax.broadcasted_iota(jnp.int32, (t, t), 1) <= lax.broadcasted_iota(jnp.int32, (t, t), 0))
    nt_dims = (((1,), (1,)), ((), ()))

    for i in range(seq // t):
        lo = i * t
        a_diag, a_past = None, None
        for mi in range(2):
            cols = slice(mi * d, (mi + 1) * d)
            qm = q_ref[lo:lo + t, cols]
            s_diag = lax.dot_general(qm, k_ref[lo:lo + t, cols], nt_dims, preferred_element_type=F32)
            s_diag = jnp.where(causal, s_diag, NEG_INF)
            mx = jnp.max(s_diag, axis=-1, keepdims=True)
            if i > 0:
                s_past = lax.dot_general(qm, k_ref[0:lo, cols], nt_dims, preferred_element_type=F32)
                mx = jnp.maximum(mx, jnp.max(s_past, axis=-1, keepdims=True))
            p_diag = jnp.exp(s_diag - mx)
            denom = jnp.sum(p_diag, axis=-1, keepdims=True)
            if i > 0:
                p_past = jnp.exp(s_past - mx)
                denom = denom + jnp.sum(p_past, axis=-1, keepdims=True)
            weight = 1.0 / denom if mi == 0 else lam / denom
            if mi == 0:
                a_diag = p_diag * weight
                a_past = p_past * weight if i > 0 else None
            else:
                a_diag = a_diag - p_diag * weight
                a_past = a_past - p_past * weight if i > 0 else None
        o = jnp.dot(a_diag.astype(BF16), v_ref[lo:lo + t, :], preferred_element_type=F32)
        if i > 0:
            o = o + jnp.dot(a_past.astype(BF16), v_ref[0:lo, :], preferred_element_type=F32)
        ms = jnp.mean(o * o, axis=-1, keepdims=True)
        o_ref[lo:lo + t, :] = ((o * lax.rsqrt(ms + EPS)) * subg_ref[...] * (1.0 - lam_init)).astype(BF16)


def _attention(z_qk, z_rest, lam_vecs, sub_g, lam_init, batch, seq):
    k_col0 = QK_WIDTH // HEAD_WIDTH
    return pl.pallas_call(
        functools.partial(_attn_kernel, lam_init=lam_init),
        grid=(batch, DIFF_HEADS),
        in_specs=[
            pl.BlockSpec((seq, HEAD_WIDTH), lambda b, h: (b, h)),
            pl.BlockSpec((seq, HEAD_WIDTH), lambda b, h: (b, k_col0 + h)),
            pl.BlockSpec((seq, HEAD_WIDTH), lambda b, h: (b, h)),
            pl.BlockSpec((4, DIFF_HEAD_DIM), lambda b, h: (0, 0)),
            pl.BlockSpec((1, HEAD_WIDTH), lambda b, h: (0, 0)),
        ],
        out_specs=pl.BlockSpec((seq, HEAD_WIDTH), lambda b, h: (b, h)),
        out_shape=jax.ShapeDtypeStruct((batch * seq, ATTN_WIDTH), BF16),
        compiler_params=_params(("parallel", "parallel")),
        name="diff_attention",
    )(z_qk, z_qk, z_rest, lam_vecs, sub_g)


def _mixers_kernel(up_ref, gb_ref, gc_ref, uc_ref, wp_ref, ps_ref, cw_ref, o_ref):
    seq = up_ref.shape[0]
    row = lax.broadcasted_iota(jnp.int32, (seq, LANES), 0)

    def delayed(x, k):
        return jnp.where(row >= k, pltpu.roll(x, k, axis=0), 0.0)

    for gi, w in enumerate(POOL_WINDOWS):
        cols = slice(gi * POOL_GROUP, (gi + 1) * POOL_GROUP)
        g = up_ref[:, cols].astype(F32)
        s, span = g, 1
        while span < w:
            s = s + delayed(s, span)
            span *= 2
        count = jnp.minimum(row + 1, w).astype(F32)
        pooled = s / count - g
        y = jnp.dot(pooled.astype(BF16), wp_ref[gi], preferred_element_type=F32) * ps_ref[:, cols]
        o_ref[:, cols] = y.astype(BF16)

    for c in range(CONV_WIDTH // LANES):
        cols = slice(c * LANES, (c + 1) * LANES)
        u = gc_ref[:, cols].astype(F32) * uc_ref[:, cols].astype(F32)
        y = (cw_ref[0:1, cols] * delayed(u, 2) + cw_ref[1:2, cols] * delayed(u, 1)) + cw_ref[2:3, cols] * u
        out = gb_ref[:, cols].astype(F32) * y
        o_ref[:, POOL_WIDTH + c * LANES:POOL_WIDTH + (c + 1) * LANES] = out.astype(BF16)


def _mixers(z, w_pool, pool_scale, conv_w, batch, seq):
    col0 = ATTN_WIDTH // POOL_WIDTH
    zspec = lambda off: pl.BlockSpec((seq, POOL_WIDTH), lambda b: (b, col0 + off))
    return pl.pallas_call(
        _mixers_kernel,
        grid=(batch,),
        in_specs=[
            zspec(0), zspec(1), zspec(2), zspec(3),
            pl.BlockSpec((len(POOL_WINDOWS), POOL_GROUP, POOL_GROUP), lambda b: (0, 0, 0)),
            pl.BlockSpec((1, POOL_WIDTH), lambda b: (0, 0)),
            pl.BlockSpec((CONV_TAPS, CONV_WIDTH), lambda b: (0, 0)),
        ],
        out_specs=pl.BlockSpec((seq, POOL_WIDTH + CONV_WIDTH), lambda b: (b, 0)),
        out_shape=jax.ShapeDtypeStruct((batch * seq, POOL_WIDTH + CONV_WIDTH), BF16),
        compiler_params=_params(("parallel",)),
        name="pool_conv_mixers",
    )(z, z, z, z, w_pool, pool_scale, conv_w)


ROUTE_GATE, ROUTE_EXPERT, ROUTE_RANK = 0, 2, 4


def _out_proj_kernel(*refs, with_router):
    if with_router:
        attn_ref, mix_ref, x_ref, wo_ref, g_ref, rw_ref, xo_ref, h_ref, route_ref, counts_ref, run_ref, r2_ref = refs
    else:
        attn_ref, mix_ref, x_ref, wo_ref, g_ref, xo_ref, h_ref = refs
    y = jnp.dot(attn_ref[...], wo_ref[:ATTN_WIDTH, :], preferred_element_type=F32)
    y = y + jnp.dot(mix_ref[...], wo_ref[ATTN_WIDTH:, :], preferred_element_type=F32)
    xn = x_ref[...] + y
    xo_ref[...] = xn
    ms = jnp.mean(xn * xn, axis=-1, keepdims=True)
    hf = xn * lax.rsqrt(ms + EPS) * g_ref[...]
    h_ref[...] = hf.astype(h_ref.dtype)
    if with_router:
        @pl.when(pl.program_id(0) == 0)
        def _():
            rw = rw_ref[...]
            r_hi = rw.astype(BF16)
            r2_ref[:, :LANES] = r_hi
            r2_ref[:, LANES:] = (rw - r_hi.astype(F32)).astype(BF16)
            run_ref[...] = jnp.zeros(run_ref.shape, F32)

        h_hi = hf.astype(BF16)
        h_lo = (hf - h_hi.astype(F32)).astype(BF16)
        hi_terms = jnp.dot(h_hi, r2_ref[...], preferred_element_type=F32)
        logits = (hi_terms[:, :LANES] + hi_terms[:, LANES:]
                  + jnp.dot(h_lo, r2_ref[:, :LANES], preferred_element_type=F32))
        tm = logits.shape[0]
        lane = lax.broadcasted_iota(jnp.int32, logits.shape, 1)
        logits = jnp.where(lane < N_EXPERTS, logits, -jnp.inf)
        v1 = jnp.max(logits, axis=-1, keepdims=True)
        i1 = jnp.min(jnp.where(logits == v1, lane, LANES), axis=-1, keepdims=True)
        rest = jnp.where(lane == i1, -jnp.inf, logits)
        v2 = jnp.max(rest, axis=-1, keepdims=True)
        i2 = jnp.min(jnp.where(rest == v2, lane, LANES), axis=-1, keepdims=True)
        e2 = jnp.exp(v2 - v1)
        g1 = 1.0 / (1.0 + e2)
        g2 = e2 / (1.0 + e2)

        sel1 = lane == i1
        sel2 = lane == i2
        chosen = jnp.where(sel1 | sel2, 1.0, 0.0)
        earlier = (lax.broadcasted_iota(jnp.int32, (tm, tm), 1) < lax.broadcasted_iota(jnp.int32, (tm, tm), 0))
        before = jnp.dot(earlier.astype(BF16), chosen.astype(BF16), preferred_element_type=F32) + run_ref[...]
        r1 = jnp.sum(jnp.where(sel1, before, 0.0), axis=-1, keepdims=True)
        r2 = jnp.sum(jnp.where(sel2, before, 0.0), axis=-1, keepdims=True)
        run_ref[...] += jnp.sum(chosen, axis=0, keepdims=True)
        counts_ref[...] = jnp.broadcast_to(run_ref[...], counts_ref.shape)

        fields = (g1, g2, i1.astype(F32), i2.astype(F32), r1, r2)
        route = jnp.zeros(logits.shape, F32)
        for k, val in enumerate(fields):
            route = jnp.where(lane == k, val, route)
        route_ref[...] = route


def _out_proj(attn, mix, x, w_out, ffn_g, router_w=None):
    n = x.shape[0]
    with_router = router_w is not None
    row_spec = lambda width: pl.BlockSpec((OUT_TM, width), lambda i: (i, 0))
    in_specs = [row_spec(ATTN_WIDTH), row_spec(POOL_WIDTH + CONV_WIDTH), row_spec(D_MODEL),
                pl.BlockSpec((D_MODEL, D_MODEL), lambda i: (0, 0)),
                pl.BlockSpec((1, D_MODEL), lambda i: (0, 0))]
    out_specs = [row_spec(D_MODEL), row_spec(D_MODEL)]
    out_shape = [jax.ShapeDtypeStruct((n, D_MODEL), F32),
                 jax.ShapeDtypeStruct((n, D_MODEL), F32 if with_router else BF16)]
    args = [attn, mix, x, w_out, ffn_g]
    scratch = []
    if with_router:
        in_specs.append(pl.BlockSpec((D_MODEL, LANES), lambda i: (0, 0)))
        out_specs += [row_spec(LANES), pl.BlockSpec((8, LANES), lambda i: (0, 0))]
        out_shape += [jax.ShapeDtypeStruct((n, LANES), F32), jax.ShapeDtypeStruct((8, LANES), F32)]
        args.append(router_w)
        scratch = [pltpu.VMEM((1, LANES), F32), pltpu.VMEM((D_MODEL, 2 * LANES), BF16)]
    return pl.pallas_call(
        functools.partial(_out_proj_kernel, with_router=with_router),
        grid=(n // OUT_TM,),
        in_specs=in_specs,
        out_specs=out_specs,
        out_shape=out_shape,
        scratch_shapes=scratch,
        compiler_params=_params(("arbitrary",) if with_router else ("parallel",)),
        name="out_proj_router" if with_router else "out_proj",
    )(*args)


def _wait_rows(src_rows, dst_rows, sem):
    pltpu.make_async_copy(src_rows, dst_rows, sem).wait()


SUBLANES = 8
PAD_CHUNK_BITS = (FFN_TM // SUBLANES // 2).bit_length()


def _pad_fill_copies(pad_start_ref, pad_len_ref, nt_ref, zero_ref, xs_ref, sem, act):
    half = zero_ref.shape[0]

    def dead_tile(t, carry):
        for part in range(FFN_TM // half):
            rows = pl.ds(pl.multiple_of(t * FFN_TM + part * half, half), half)
            act(pltpu.make_async_copy(zero_ref, xs_ref.at[rows], sem))
        return carry

    lax.fori_loop(nt_ref[0], xs_ref.shape[0] // FFN_TM, dead_tile, 0)
    for e in range(N_EXPERTS):
        start, length = pad_start_ref[e], pad_len_ref[e]
        head = jnp.minimum((SUBLANES - start % SUBLANES) % SUBLANES, length)
        for i in range(SUBLANES - 1):
            @pl.when(i < head)
            def _():
                act(pltpu.make_async_copy(zero_ref.at[pl.ds(0, 1)], xs_ref.at[pl.ds(start + i, 1)], sem))

        aligned = start + head
        groups = (length - head) // SUBLANES
        for b in reversed(range(PAD_CHUNK_BITS)):
            size = SUBLANES << b

            @pl.when(((groups >> b) & 1) == 1)
            def _():
                covered = ((groups >> (b + 1)) << (b + 1)) * SUBLANES
                rows = pl.ds(pl.multiple_of(aligned + covered, SUBLANES), size)
                act(pltpu.make_async_copy(zero_ref.at[pl.ds(0, size)], xs_ref.at[rows], sem))


def _dispatch_kernel(slot_ref, pad_start_ref, pad_len_ref, nt_ref, h_ref, xs_ref, zero_ref, sem, pad_sem):
    rows = h_ref.shape[0]
    base = pl.program_id(0) * rows * TOP_K
    pad_args = (pad_start_ref, pad_len_ref, nt_ref, zero_ref, xs_ref, pad_sem)

    @pl.when(pl.program_id(0) == 0)
    def _():
        zero_ref[...] = jnp.zeros(zero_ref.shape, F32)
        _pad_fill_copies(*pad_args, lambda copy: copy.start())

    def issue(r, carry):
        for k in range(TOP_K):
            s = slot_ref[base + r * TOP_K + k]
            pltpu.make_async_copy(h_ref.at[pl.ds(r, 1)], xs_ref.at[pl.ds(s, 1)], sem).start()
        return carry

    lax.fori_loop(0, rows, issue, 0, unroll=8)
    for _ in range(TOP_K):
        _wait_rows(h_ref, xs_ref.at[pl.ds(0, rows)], sem)

    @pl.when(pl.program_id(0) == 0)
    def _():
        _pad_fill_copies(*pad_args, lambda copy: copy.wait())


def _dispatch(slot, pad_start, pad_len, n_tiles, h, n_slots):
    n = h.shape[0]
    grid_spec = pltpu.PrefetchScalarGridSpec(
        num_scalar_prefetch=4,
        grid=(n // PERM_TM,),
        in_specs=[pl.BlockSpec((PERM_TM, D_MODEL), lambda i, s, ps, pn, nt: (i, 0))],
        out_specs=pl.BlockSpec(memory_space=pl.ANY),
        scratch_shapes=[pltpu.VMEM((FFN_TM // 2, D_MODEL), F32), pltpu.SemaphoreType.DMA(()),
                        pltpu.SemaphoreType.DMA(())],
    )
    return pl.pallas_call(
        _dispatch_kernel,
        grid_spec=grid_spec,
        out_shape=jax.ShapeDtypeStruct((n_slots, D_MODEL), F32),
        compiler_params=_params(("arbitrary",)),
        name="moe_dispatch",
    )(slot, pad_start, pad_len, n_tiles, h)


def _combine_kernel(slot_ref, x_ref, route_ref, y_ref, o_ref, rows_ref, sem):
    rows = x_ref.shape[0]
    base = pl.program_id(0) * rows * TOP_K

    def issue(r, carry):
        for k in range(TOP_K):
            s = slot_ref[base + r * TOP_K + k]
            pltpu.make_async_copy(y_ref.at[pl.ds(s, 1)], rows_ref.at[k, pl.ds(r, 1)], sem).start()
        return carry

    lax.fori_loop(0, rows, issue, 0, unroll=8)
    for k in range(TOP_K):
        _wait_rows(y_ref.at[pl.ds(0, rows)], rows_ref.at[k], sem)
    route = route_ref[...]
    out = x_ref[...]
    for k in range(TOP_K):
        out = out + route[:, ROUTE_GATE + k:ROUTE_GATE + k + 1] * rows_ref[k]
    o_ref[...] = out


def _combine(slot, x, route, y):
    n = x.shape[0]
    grid_spec = pltpu.PrefetchScalarGridSpec(
        num_scalar_prefetch=1,
        grid=(n // PERM_TM,),
        in_specs=[
            pl.BlockSpec((PERM_TM, D_MODEL), lambda i, s: (i, 0)),
            pl.BlockSpec((PERM_TM, LANES), lambda i, s: (i, 0)),
            pl.BlockSpec(memory_space=pl.ANY),
        ],
        out_specs=pl.BlockSpec((PERM_TM, D_MODEL), lambda i, s: (i, 0)),
        scratch_shapes=[pltpu.VMEM((TOP_K, PERM_TM, D_MODEL), F32), pltpu.SemaphoreType.DMA(())],
    )
    return pl.pallas_call(
        _combine_kernel,
        grid_spec=grid_spec,
        out_shape=jax.ShapeDtypeStruct((n, D_MODEL), F32),
        compiler_params=_params(("arbitrary",)),
        name="moe_combine",
    )(slot, x, route, y)


def _ffn_steps(f, n_f, rows, x_ref, wg_ref, wu_ref, wd_ref, act_refs, accumulate):
    def activate(dst_ref):
        x = x_ref[rows, :]
        g = jnp.dot(x, wg_ref[0].astype(BF16), preferred_element_type=F32)
        u = jnp.dot(x, wu_ref[0].astype(BF16), preferred_element_type=F32)
        dst_ref[rows, :] = (g * jax.nn.sigmoid(g) * u).astype(BF16)

    def project_down(src_ref):
        accumulate(rows, jnp.dot(src_ref[rows, :], wd_ref[0].astype(BF16), preferred_element_type=F32))

    @pl.when(f == 0)
    def _():
        activate(act_refs[0])

    for parity in range(2):
        @pl.when((f > 0) & (f < n_f) & (f % 2 == parity))
        def _():
            activate(act_refs[parity])
            project_down(act_refs[1 - parity])

    @pl.when(f == n_f)
    def _():
        project_down(act_refs[(n_f - 1) % 2])


def _dense_ffn_kernel(x_ref, wg_ref, wu_ref, wd_ref, res_ref, o_ref, acc_ref, act0_ref, act1_ref, *, n_f):
    f = pl.program_id(1)

    @pl.when(f == 0)
    def _():
        acc_ref[...] = jnp.zeros(acc_ref.shape, F32)

    def accumulate(rows, update):
        acc_ref[rows, :] += update

    _ffn_steps(f, n_f, slice(0, FFN_TM), x_ref, wg_ref, wu_ref, wd_ref, (act0_ref, act1_ref), accumulate)

    @pl.when(f == n_f)
    def _():
        o_ref[...] = res_ref[...] + acc_ref[...]


def _act_scratch():
    return [pltpu.VMEM((FFN_TM, FFN_TF), BF16), pltpu.VMEM((FFN_TM, FFN_TF), BF16)]


def _dense_ffn(h, wg, wu, wd, res):
    rows = h.shape[0]
    n_f = wg.shape[-1] // FFN_TF
    once = dict(pipeline_mode=pl.Buffered(1))
    gate_up = lambda t, f: (0, 0, jnp.minimum(f, n_f - 1))
    return pl.pallas_call(
        functools.partial(_dense_ffn_kernel, n_f=n_f),
        grid=(rows // FFN_TM, n_f + 1),
        in_specs=[
            pl.BlockSpec((FFN_TM, D_MODEL), lambda t, f: (t, 0)),
            pl.BlockSpec((1, D_MODEL, FFN_TF), gate_up),
            pl.BlockSpec((1, D_MODEL, FFN_TF), gate_up),
            pl.BlockSpec((1, FFN_TF, D_MODEL), lambda t, f: (0, jnp.maximum(f - 1, 0), 0)),
            pl.BlockSpec((FFN_TM, D_MODEL), lambda t, f: (t, 0), **once),
        ],
        out_specs=pl.BlockSpec((FFN_TM, D_MODEL), lambda t, f: (t, 0), **once),
        out_shape=jax.ShapeDtypeStruct((rows, D_MODEL), F32),
        scratch_shapes=[pltpu.VMEM((FFN_TM, D_MODEL), F32)] + _act_scratch(),
        compiler_params=_params(("parallel", "arbitrary")),
        name="dense_ffn",
    )(h, wg, wu, wd, res)


def _moe_ffn_kernel(te_ref, nt_ref, tv_ref, x_ref, wg_ref, wu_ref, wd_ref, o_ref, xb_ref, act0_ref, act1_ref, *, n_f):
    t = pl.program_id(0)
    f = pl.program_id(1)
    live = t < nt_ref[0]
    valid = tv_ref[t]

    @pl.when(f == 0)
    def _():
        o_ref[...] = jnp.zeros(o_ref.shape, F32)

        @pl.when(live)
        def _():
            xb_ref[...] = x_ref[...].astype(BF16)

    def accumulate(rows, update):
        o_ref[rows, :] += update

    for rows, applies in ((slice(0, FFN_TM), valid > FFN_TM // 2), (slice(0, FFN_TM // 2), valid <= FFN_TM // 2)):
        @pl.when(live & applies)
        def _():
            _ffn_steps(f, n_f, rows, xb_ref, wg_ref, wu_ref, wd_ref, (act0_ref, act1_ref), accumulate)


def _moe_ffn(tile_expert, n_tiles, tile_valid, x_sorted, wg, wu, wd):
    rows = x_sorted.shape[0]
    n_f = wg.shape[-1] // FFN_TF

    def x_tile(t, f, te, nt, tv):
        return (jnp.minimum(t, nt[0] - 1), 0)

    def gate_up(t, f, te, nt, tv):
        return (te[t], 0, jnp.where(t < nt[0], jnp.minimum(f, n_f - 1), n_f - 1))

    def down(t, f, te, nt, tv):
        return (te[t], jnp.where(t < nt[0], jnp.maximum(f - 1, 0), n_f - 1), 0)

    grid_spec = pltpu.PrefetchScalarGridSpec(
        num_scalar_prefetch=3,
        grid=(rows // FFN_TM, n_f + 1),
        in_specs=[
            pl.BlockSpec((FFN_TM, D_MODEL), x_tile),
            pl.BlockSpec((1, D_MODEL, FFN_TF), gate_up),
            pl.BlockSpec((1, D_MODEL, FFN_TF), gate_up),
            pl.BlockSpec((1, FFN_TF, D_MODEL), down),
        ],
        out_specs=pl.BlockSpec((FFN_TM, D_MODEL), lambda t, f, te, nt, tv: (t, 0), pipeline_mode=pl.Buffered(1)),
        scratch_shapes=[pltpu.VMEM((FFN_TM, D_MODEL), BF16)] + _act_scratch(),
    )
    return pl.pallas_call(
        functools.partial(_moe_ffn_kernel, n_f=n_f),
        grid_spec=grid_spec,
        out_shape=jax.ShapeDtypeStruct((rows, D_MODEL), F32),
        compiler_params=_params(("arbitrary", "arbitrary")),
        name="moe_ffn",
    )(tile_expert, n_tiles, tile_valid, x_sorted, wg, wu, wd)


def _moe_layer(x, h, route, counts, wg, wu, wd):
    n = x.shape[0]
    n_t = (n * TOP_K) // FFN_TM + N_EXPERTS
    experts = jnp.arange(N_EXPERTS, dtype=jnp.int32)
    counts = counts[0, :N_EXPERTS].astype(jnp.int32)
    tiles_per = (counts + FFN_TM - 1) // FFN_TM
    tiles_end = jnp.cumsum(tiles_per)
    first_tile = tiles_end - tiles_per
    idx = route[:, ROUTE_EXPERT:ROUTE_EXPERT + TOP_K].astype(jnp.int32)
    rank = route[:, ROUTE_RANK:ROUTE_RANK + TOP_K].astype(jnp.int32)
    group_start = jnp.sum(jnp.where(idx[:, :, None] == experts, first_tile * FFN_TM, 0), axis=-1)
    slot = (group_start + rank).reshape(-1)

    n_tiles = tiles_end[-1:]
    t_ids = jnp.minimum(jnp.arange(n_t, dtype=jnp.int32), n_tiles[0] - 1)
    tile_expert = jnp.minimum(jnp.sum((t_ids[:, None] >= tiles_end[None, :]).astype(jnp.int32), axis=1),
                              N_EXPERTS - 1)
    of_tile = tile_expert[:, None] == experts[None, :]
    tile_valid = jnp.clip(jnp.sum(jnp.where(of_tile, counts - (t_ids[:, None] - first_tile) * FFN_TM, 0), axis=1),
                          0, FFN_TM)
    pad_start = first_tile * FFN_TM + counts
    pad_len = tiles_per * FFN_TM - counts

    x_sorted = _dispatch(slot, pad_start, pad_len, n_tiles, h, n_t * FFN_TM)
    y = _moe_ffn(tile_expert, n_tiles, tile_valid, x_sorted, wg, wu, wd)
    return _combine(slot, x, route, y)


def kernel(x, positions, attn_norm_g, w_in, q_norm_g, k_norm_g, lambda_vecs, attn_out_norm_g, w_pool, pool_scale,
           conv_w, w_out, ffn_norm_g, dense_w_gate, dense_w_up, dense_w_down, router_w, moe_w_gate, moe_w_up,
           moe_w_down):
    batch, seq, d_model = x.shape
    depth = w_in.shape[0]
    n = batch * seq

    inv_freq = 1.0 / (ROPE_THETA ** (jnp.arange(0, DIFF_HEAD_DIM, 2, dtype=F32) / DIFF_HEAD_DIM))
    ang = positions.astype(F32)[..., None] * inv_freq
    cos = jnp.cos(ang).astype(x.dtype).reshape(n, DIFF_HEAD_DIM // 2)
    sin = jnp.sin(ang).astype(x.dtype).reshape(n, DIFF_HEAD_DIM // 2)
    cos2 = jnp.concatenate([cos, cos], axis=-1)
    sin2 = jnp.concatenate([-sin, sin], axis=-1)

    xf = x.reshape(n, d_model)
    for l in range(depth):
        lam_init = 0.8 - 0.6 * math.exp(-0.3 * l)
        qk_g = jnp.stack([q_norm_g[l], k_norm_g[l]])[:, None, :]
        z_qk, z_rest = _in_proj(xf, attn_norm_g[l][None, :], w_in[l].astype(BF16), cos2, sin2, qk_g)
        attn = _attention(z_qk, z_rest, lambda_vecs[l], attn_out_norm_g[l][None, :], lam_init, batch, seq)
        mix = _mixers(z_rest, w_pool[l].astype(BF16), pool_scale[l][None, :], conv_w[l], batch, seq)
        j = l // 2
        if l % 2 == 0:
            xf, h = _out_proj(attn, mix, xf, w_out[l].astype(BF16), ffn_norm_g[l][None, :])
            xf = _dense_ffn(h, dense_w_gate[j][None], dense_w_up[j][None], dense_w_down[j][None], xf)
        else:
            rw = jnp.pad(router_w[j], ((0, 0), (0, LANES - N_EXPERTS)))
            xf, h, route, counts = _out_proj(attn, mix, xf, w_out[l].astype(BF16), ffn_norm_g[l][None, :], rw)
            xf = _moe_layer(xf, h, route, counts, moe_w_gate[j], moe_w_up[j], moe_w_down[j])
    return xf.reshape(batch, seq, d_model)
```

```python
import functools
import math

import jax
import jax.numpy as jnp
from jax import lax
from jax.experimental import pallas as pl
from jax.experimental.pallas import tpu as pltpu

D_MODEL = 2048
DIFF_HEAD_DIM = 128
DIFF_HEADS = 4
HEAD_WIDTH = 2 * DIFF_HEAD_DIM
QK_WIDTH = DIFF_HEADS * HEAD_WIDTH
ATTN_WIDTH = QK_WIDTH
POOL_WINDOWS = (2, 4, 8, 16)
POOL_GROUP = 128
POOL_WIDTH = 512
CONV_WIDTH = 512
CONV_TAPS = 3
IN_WIDTH = 5120
ROPE_THETA = 10000.0
N_EXPERTS = 8
TOP_K = 2
EPS = 1e-6
NEG_INF = -1e30
LANES = 128
Q_SCALE = DIFF_HEAD_DIM ** -0.5 * math.log2(math.e)

BF16 = jnp.bfloat16
F32 = jnp.float32

VMEM_LIMIT_BYTES = 56 * 1024 * 1024

IN_TM = 1024
IN_TN = 1024
ATTN_TQ = 512
OUT_TM = 512
FFN_TM = 1024
FFN_TF = 256
PERM_TM = 512


def _params(semantics):
    return pltpu.CompilerParams(dimension_semantics=semantics, vmem_limit_bytes=VMEM_LIMIT_BYTES)


def _in_proj_kernel(x_ref, g_ref, w_ref, cos_ref, sin_ref, qkg_ref, zqk_ref, zrest_ref, h_ref, rawq_ref, rawk_ref):
    j = pl.program_id(1)

    def project():
        return jnp.dot(h_ref[...], w_ref[...], preferred_element_type=F32)

    def head_norm_rotary(raw_ref, which):
        gain = qkg_ref[which]
        cos = cos_ref[...]
        sin = sin_ref[...]
        for c in range(IN_TN // DIFF_HEAD_DIM):
            cols = slice(c * DIFF_HEAD_DIM, (c + 1) * DIFF_HEAD_DIM)
            zc = raw_ref[:, cols]
            ms = jnp.mean(zc * zc, axis=-1, keepdims=True)
            zn = zc * lax.rsqrt(ms + EPS) * gain
            rot = zn * cos + pltpu.roll(zn, DIFF_HEAD_DIM // 2, axis=1) * sin
            if which == 0:
                rot = rot * Q_SCALE
            zqk_ref[:, cols] = rot.astype(BF16)

    @pl.when(j == 0)
    def _():
        x = x_ref[...]
        ms = jnp.mean(x * x, axis=-1, keepdims=True)
        h_ref[...] = (x * lax.rsqrt(ms + EPS) * g_ref[...]).astype(BF16)
        rawq_ref[...] = project()

    @pl.when(j == 1)
    def _():
        rawk_ref[...] = project()
        head_norm_rotary(rawq_ref, 0)

    @pl.when(j == 2)
    def _():
        zrest_ref[...] = project().astype(BF16)
        head_norm_rotary(rawk_ref, 1)

    @pl.when(j > 2)
    def _():
        zrest_ref[...] = project().astype(BF16)


def _in_proj(x, norm_g, w_in, cos2, sin2, qk_g):
    n = x.shape[0]
    n_qk = 2 * QK_WIDTH // IN_TN
    grid = (n // IN_TM, IN_WIDTH // IN_TN)
    return pl.pallas_call(
        _in_proj_kernel,
        grid=grid,
        in_specs=[
            pl.BlockSpec((IN_TM, D_MODEL), lambda i, j: (i, 0)),
            pl.BlockSpec((1, D_MODEL), lambda i, j: (0, 0)),
            pl.BlockSpec((D_MODEL, IN_TN), lambda i, j: (0, j)),
            pl.BlockSpec((IN_TM, DIFF_HEAD_DIM), lambda i, j: (i, 0)),
            pl.BlockSpec((IN_TM, DIFF_HEAD_DIM), lambda i, j: (i, 0)),
            pl.BlockSpec((2, 1, DIFF_HEAD_DIM), lambda i, j: (0, 0, 0)),
        ],
        out_specs=[
            pl.BlockSpec((IN_TM, IN_TN), lambda i, j: (i, jnp.clip(j - 1, 0, n_qk - 1))),
            pl.BlockSpec((IN_TM, IN_TN), lambda i, j: (i, jnp.maximum(j - n_qk, 0))),
        ],
        out_shape=[jax.ShapeDtypeStruct((n, 2 * QK_WIDTH), BF16),
                   jax.ShapeDtypeStruct((n, IN_WIDTH - 2 * QK_WIDTH), BF16)],
        scratch_shapes=[pltpu.VMEM((IN_TM, D_MODEL), BF16), pltpu.VMEM((IN_TM, IN_TN), F32),
                        pltpu.VMEM((IN_TM, IN_TN), F32)],
        compiler_params=_params(("parallel", "arbitrary")),
        name="in_proj",
    )(x, norm_g, w_in, cos2, sin2, qk_g)


def _attn_kernel(q_ref, k_ref, v_ref, lv_ref, subg_ref, o_ref, *, lam_init):
    t, d = ATTN_TQ, DIFF_HEAD_DIM
    seq = q_ref.shape[0]
    lv = lv_ref[...]
    lam = (jnp.exp(jnp.sum(lv[0:1] * lv[1:2], axis=-1, keepdims=True))
           - jnp.exp(jnp.sum(lv[2:3] * lv[3:4], axis=-1, keepdims=True)) + lam_init)
    causal = (lax.broadcasted_iota(jnp.int32, (t, t), 1) <= lax.broadcasted_iota(jnp.int32, (t, t), 0))
    nt_dims = (((1,), (1,)), ((), ()))

    for i in range(seq // t):
        lo = i * t
        a_diag, a_past = None, None
        for mi in range(2):
            cols = slice(mi * d, (mi + 1) * d)
            qm = q_ref[lo:lo + t, cols]
            s_diag = lax.dot_general(qm, k_ref[lo:lo + t, cols], nt_dims, preferred_element_type=F32)
            s_diag = jnp.where(causal, s_diag, NEG_INF)
            mx = jnp.max(s_diag, axis=-1, keepdims=True)
            if i > 0:
                s_past = lax.dot_general(qm, k_ref[0:lo, cols], nt_dims, preferred_element_type=F32)
                mx = jnp.maximum(mx, jnp.max(s_past, axis=-1, keepdims=True))
            p_diag = jnp.exp2(s_diag - mx)
            denom = jnp.sum(p_diag, axis=-1, keepdims=True)
            if i > 0:
                p_past = jnp.exp2(s_past - mx)
                denom = denom + jnp.sum(p_past, axis=-1, keepdims=True)
            weight = 1.0 / denom if mi == 0 else lam / denom
            if mi == 0:
                a_diag = p_diag * weight
                a_past = p_past * weight if i > 0 else None
            else:
                a_diag = a_diag - p_diag * weight
                a_past = a_past - p_past * weight if i > 0 else None
        o = jnp.dot(a_diag.astype(BF16), v_ref[lo:lo + t, :], preferred_element_type=F32)
        if i > 0:
            o = o + jnp.dot(a_past.astype(BF16), v_ref[0:lo, :], preferred_element_type=F32)
        ms = jnp.mean(o * o, axis=-1, keepdims=True)
        o_ref[lo:lo + t, :] = ((o * lax.rsqrt(ms + EPS)) * subg_ref[...] * (1.0 - lam_init)).astype(BF16)


def _attention(z_qk, z_rest, lam_vecs, sub_g, lam_init, batch, seq):
    k_col0 = QK_WIDTH // HEAD_WIDTH
    return pl.pallas_call(
        functools.partial(_attn_kernel, lam_init=lam_init),
        grid=(batch, DIFF_HEADS),
        in_specs=[
            pl.BlockSpec((seq, HEAD_WIDTH), lambda b, h: (b, h)),
            pl.BlockSpec((seq, HEAD_WIDTH), lambda b, h: (b, k_col0 + h)),
            pl.BlockSpec((seq, HEAD_WIDTH), lambda b, h: (b, h)),
            pl.BlockSpec((4, DIFF_HEAD_DIM), lambda b, h: (0, 0)),
            pl.BlockSpec((1, HEAD_WIDTH), lambda b, h: (0, 0)),
        ],
        out_specs=pl.BlockSpec((seq, HEAD_WIDTH), lambda b, h: (b, h)),
        out_shape=jax.ShapeDtypeStruct((batch * seq, ATTN_WIDTH), BF16),
        compiler_params=_params(("parallel", "parallel")),
        name="diff_attention",
    )(z_qk, z_qk, z_rest, lam_vecs, sub_g)


def _mixers_kernel(up_ref, gb_ref, gc_ref, uc_ref, wp_ref, ps_ref, cw_ref, o_ref):
    seq = up_ref.shape[0]
    row = lax.broadcasted_iota(jnp.int32, (seq, LANES), 0)

    def delayed(x, k):
        return jnp.where(row >= k, pltpu.roll(x, k, axis=0), 0.0)

    for gi, w in enumerate(POOL_WINDOWS):
        cols = slice(gi * POOL_GROUP, (gi + 1) * POOL_GROUP)
        g = up_ref[:, cols].astype(F32)
        s, span = g, 1
        while span < w:
            s = s + delayed(s, span)
            span *= 2
        count = jnp.minimum(row + 1, w).astype(F32)
        pooled = s / count - g
        y = jnp.dot(pooled.astype(BF16), wp_ref[gi], preferred_element_type=F32) * ps_ref[:, cols]
        o_ref[:, cols] = y.astype(BF16)

    for c in range(CONV_WIDTH // LANES):
        cols = slice(c * LANES, (c + 1) * LANES)
        u = gc_ref[:, cols].astype(F32) * uc_ref[:, cols].astype(F32)
        y = (cw_ref[0:1, cols] * delayed(u, 2) + cw_ref[1:2, cols] * delayed(u, 1)) + cw_ref[2:3, cols] * u
        out = gb_ref[:, cols].astype(F32) * y
        o_ref[:, POOL_WIDTH + c * LANES:POOL_WIDTH + (c + 1) * LANES] = out.astype(BF16)


def _mixers(z, w_pool, pool_scale, conv_w, batch, seq):
    col0 = ATTN_WIDTH // POOL_WIDTH
    zspec = lambda off: pl.BlockSpec((seq, POOL_WIDTH), lambda b: (b, col0 + off))
    return pl.pallas_call(
        _mixers_kernel,
        grid=(batch,),
        in_specs=[
            zspec(0), zspec(1), zspec(2), zspec(3),
            pl.BlockSpec((len(POOL_WINDOWS), POOL_GROUP, POOL_GROUP), lambda b: (0, 0, 0)),
            pl.BlockSpec((1, POOL_WIDTH), lambda b: (0, 0)),
            pl.BlockSpec((CONV_TAPS, CONV_WIDTH), lambda b: (0, 0)),
        ],
        out_specs=pl.BlockSpec((seq, POOL_WIDTH + CONV_WIDTH), lambda b: (b, 0)),
        out_shape=jax.ShapeDtypeStruct((batch * seq, POOL_WIDTH + CONV_WIDTH), BF16),
        compiler_params=_params(("parallel",)),
        name="pool_conv_mixers",
    )(z, z, z, z, w_pool, pool_scale, conv_w)


ROUTE_GATE, ROUTE_EXPERT, ROUTE_RANK = 0, 2, 4


def _out_proj_kernel(*refs, with_router):
    if with_router:
        attn_ref, mix_ref, x_ref, wo_ref, g_ref, rw_ref, xo_ref, h_ref, route_ref, counts_ref, run_ref, r2_ref = refs
    else:
        attn_ref, mix_ref, x_ref, wo_ref, g_ref, xo_ref, h_ref = refs
    y = jnp.dot(attn_ref[...], wo_ref[:ATTN_WIDTH, :], preferred_element_type=F32)
    y = y + jnp.dot(mix_ref[...], wo_ref[ATTN_WIDTH:, :], preferred_element_type=F32)
    xn = x_ref[...] + y
    xo_ref[...] = xn
    ms = jnp.mean(xn * xn, axis=-1, keepdims=True)
    hf = xn * lax.rsqrt(ms + EPS) * g_ref[...]
    h_ref[...] = hf.astype(h_ref.dtype)
    if with_router:
        @pl.when(pl.program_id(0) == 0)
        def _():
            rw = rw_ref[...]
            r_hi = rw.astype(BF16)
            r2_ref[:, :LANES] = r_hi
            r2_ref[:, LANES:] = (rw - r_hi.astype(F32)).astype(BF16)
            run_ref[...] = jnp.zeros(run_ref.shape, F32)

        h_hi = hf.astype(BF16)
        h_lo = (hf - h_hi.astype(F32)).astype(BF16)
        hi_terms = jnp.dot(h_hi, r2_ref[...], preferred_element_type=F32)
        logits = (hi_terms[:, :LANES] + hi_terms[:, LANES:]
                  + jnp.dot(h_lo, r2_ref[:, :LANES], preferred_element_type=F32))
        tm = logits.shape[0]
        lane = lax.broadcasted_iota(jnp.int32, logits.shape, 1)
        logits = jnp.where(lane < N_EXPERTS, logits, -jnp.inf)
        v1 = jnp.max(logits, axis=-1, keepdims=True)
        i1 = jnp.min(jnp.where(logits == v1, lane, LANES), axis=-1, keepdims=True)
        rest = jnp.where(lane == i1, -jnp.inf, logits)
        v2 = jnp.max(rest, axis=-1, keepdims=True)
        i2 = jnp.min(jnp.where(rest == v2, lane, LANES), axis=-1, keepdims=True)
        e2 = jnp.exp(v2 - v1)
        g1 = 1.0 / (1.0 + e2)
        g2 = e2 / (1.0 + e2)

        sel1 = lane == i1
        sel2 = lane == i2
        chosen = jnp.where(sel1 | sel2, 1.0, 0.0)
        earlier = (lax.broadcasted_iota(jnp.int32, (tm, tm), 1) < lax.broadcasted_iota(jnp.int32, (tm, tm), 0))
        before = jnp.dot(earlier.astype(BF16), chosen.astype(BF16), preferred_element_type=F32) + run_ref[...]
        r1 = jnp.sum(jnp.where(sel1, before, 0.0), axis=-1, keepdims=True)
        r2 = jnp.sum(jnp.where(sel2, before, 0.0), axis=-1, keepdims=True)
        run_ref[...] += jnp.sum(chosen, axis=0, keepdims=True)
        counts_ref[...] = jnp.broadcast_to(run_ref[...], counts_ref.shape)

        fields = (g1, g2, i1.astype(F32), i2.astype(F32), r1, r2)
        route = jnp.zeros(logits.shape, F32)
        for k, val in enumerate(fields):
            route = jnp.where(lane == k, val, route)
        route_ref[...] = route


def _out_proj(attn, mix, x, w_out, ffn_g, router_w=None):
    n = x.shape[0]
    with_router = router_w is not None
    row_spec = lambda width: pl.BlockSpec((OUT_TM, width), lambda i: (i, 0))
    in_specs = [row_spec(ATTN_WIDTH), row_spec(POOL_WIDTH + CONV_WIDTH), row_spec(D_MODEL),
                pl.BlockSpec((D_MODEL, D_MODEL), lambda i: (0, 0)),
                pl.BlockSpec((1, D_MODEL), lambda i: (0, 0))]
    out_specs = [row_spec(D_MODEL), row_spec(D_MODEL)]
    out_shape = [jax.ShapeDtypeStruct((n, D_MODEL), F32),
                 jax.ShapeDtypeStruct((n, D_MODEL), F32 if with_router else BF16)]
    args = [attn, mix, x, w_out, ffn_g]
    scratch = []
    if with_router:
        in_specs.append(pl.BlockSpec((D_MODEL, LANES), lambda i: (0, 0)))
        out_specs += [row_spec(LANES), pl.BlockSpec((8, LANES), lambda i: (0, 0))]
        out_shape += [jax.ShapeDtypeStruct((n, LANES), F32), jax.ShapeDtypeStruct((8, LANES), F32)]
        args.append(router_w)
        scratch = [pltpu.VMEM((1, LANES), F32), pltpu.VMEM((D_MODEL, 2 * LANES), BF16)]
    return pl.pallas_call(
        functools.partial(_out_proj_kernel, with_router=with_router),
        grid=(n // OUT_TM,),
        in_specs=in_specs,
        out_specs=out_specs,
        out_shape=out_shape,
        scratch_shapes=scratch,
        compiler_params=_params(("arbitrary",) if with_router else ("parallel",)),
        name="out_proj_router" if with_router else "out_proj",
    )(*args)


def _wait_rows(src_rows, dst_rows, sem):
    pltpu.make_async_copy(src_rows, dst_rows, sem).wait()


SUBLANES = 8
PAD_CHUNK_BITS = (FFN_TM // SUBLANES // 2).bit_length()


def _pad_fill_copies(pad_start_ref, pad_len_ref, nt_ref, zero_ref, xs_ref, sem, act):
    half = zero_ref.shape[0]

    def dead_tile(t, carry):
        for part in range(FFN_TM // half):
            rows = pl.ds(pl.multiple_of(t * FFN_TM + part * half, half), half)
            act(pltpu.make_async_copy(zero_ref, xs_ref.at[rows], sem))
        return carry

    lax.fori_loop(nt_ref[0], xs_ref.shape[0] // FFN_TM, dead_tile, 0)
    for e in range(N_EXPERTS):
        start, length = pad_start_ref[e], pad_len_ref[e]
        head = jnp.minimum((SUBLANES - start % SUBLANES) % SUBLANES, length)
        for i in range(SUBLANES - 1):
            @pl.when(i < head)
            def _():
                act(pltpu.make_async_copy(zero_ref.at[pl.ds(0, 1)], xs_ref.at[pl.ds(start + i, 1)], sem))

        aligned = start + head
        groups = (length - head) // SUBLANES
        for b in reversed(range(PAD_CHUNK_BITS)):
            size = SUBLANES << b

            @pl.when(((groups >> b) & 1) == 1)
            def _():
                covered = ((groups >> (b + 1)) << (b + 1)) * SUBLANES
                rows = pl.ds(pl.multiple_of(aligned + covered, SUBLANES), size)
                act(pltpu.make_async_copy(zero_ref.at[pl.ds(0, size)], xs_ref.at[rows], sem))


def _dispatch_kernel(slot_ref, pad_start_ref, pad_len_ref, nt_ref, h_ref, xs_ref, zero_ref, sem, pad_sem):
    rows = h_ref.shape[0]
    base = pl.program_id(0) * rows * TOP_K
    pad_args = (pad_start_ref, pad_len_ref, nt_ref, zero_ref, xs_ref, pad_sem)

    @pl.when(pl.program_id(0) == 0)
    def _():
        zero_ref[...] = jnp.zeros(zero_ref.shape, F32)
        _pad_fill_copies(*pad_args, lambda copy: copy.start())

    def issue(r, carry):
        for k in range(TOP_K):
            s = slot_ref[base + r * TOP_K + k]
            pltpu.make_async_copy(h_ref.at[pl.ds(r, 1)], xs_ref.at[pl.ds(s, 1)], sem).start()
        return carry

    lax.fori_loop(0, rows, issue, 0, unroll=8)
    for _ in range(TOP_K):
        _wait_rows(h_ref, xs_ref.at[pl.ds(0, rows)], sem)

    @pl.when(pl.program_id(0) == 0)
    def _():
        _pad_fill_copies(*pad_args, lambda copy: copy.wait())


def _dispatch(slot, pad_start, pad_len, n_tiles, h, n_slots):
    n = h.shape[0]
    grid_spec = pltpu.PrefetchScalarGridSpec(
        num_scalar_prefetch=4,
        grid=(n // PERM_TM,),
        in_specs=[pl.BlockSpec((PERM_TM, D_MODEL), lambda i, s, ps, pn, nt: (i, 0))],
        out_specs=pl.BlockSpec(memory_space=pl.ANY),
        scratch_shapes=[pltpu.VMEM((FFN_TM // 2, D_MODEL), F32), pltpu.SemaphoreType.DMA(()),
                        pltpu.SemaphoreType.DMA(())],
    )
    return pl.pallas_call(
        _dispatch_kernel,
        grid_spec=grid_spec,
        out_shape=jax.ShapeDtypeStruct((n_slots, D_MODEL), F32),
        compiler_params=_params(("arbitrary",)),
        name="moe_dispatch",
    )(slot, pad_start, pad_len, n_tiles, h)


def _combine_kernel(slot_ref, x_ref, route_ref, y_ref, o_ref, rows_ref, sem):
    rows = x_ref.shape[0]
    base = pl.program_id(0) * rows * TOP_K

    def issue(r, carry):
        for k in range(TOP_K):
            s = slot_ref[base + r * TOP_K + k]
            pltpu.make_async_copy(y_ref.at[pl.ds(s, 1)], rows_ref.at[k, pl.ds(r, 1)], sem).start()
        return carry

    lax.fori_loop(0, rows, issue, 0, unroll=8)
    for k in range(TOP_K):
        _wait_rows(y_ref.at[pl.ds(0, rows)], rows_ref.at[k], sem)
    route = route_ref[...]
    out = x_ref[...]
    for k in range(TOP_K):
        out = out + route[:, ROUTE_GATE + k:ROUTE_GATE + k + 1] * rows_ref[k]
    o_ref[...] = out


def _combine(slot, x, route, y):
    n = x.shape[0]
    grid_spec = pltpu.PrefetchScalarGridSpec(
        num_scalar_prefetch=1,
        grid=(n // PERM_TM,),
        in_specs=[
            pl.BlockSpec((PERM_TM, D_MODEL), lambda i, s: (i, 0)),
            pl.BlockSpec((PERM_TM, LANES), lambda i, s: (i, 0)),
            pl.BlockSpec(memory_space=pl.ANY),
        ],
        out_specs=pl.BlockSpec((PERM_TM, D_MODEL), lambda i, s: (i, 0)),
        scratch_shapes=[pltpu.VMEM((TOP_K, PERM_TM, D_MODEL), F32), pltpu.SemaphoreType.DMA(())],
    )
    return pl.pallas_call(
        _combine_kernel,
        grid_spec=grid_spec,
        out_shape=jax.ShapeDtypeStruct((n, D_MODEL), F32),
        compiler_params=_params(("arbitrary",)),
        name="moe_combine",
    )(slot, x, route, y)


def _swiglu_rows(x, wg_ref, wu_ref, wd_ref):
    g = jnp.dot(x, wg_ref[0].astype(BF16), preferred_element_type=F32)
    u = jnp.dot(x, wu_ref[0].astype(BF16), preferred_element_type=F32)
    a = (g * jax.nn.sigmoid(g) * u).astype(BF16)
    return jnp.dot(a, wd_ref[0].astype(BF16), preferred_element_type=F32)


def _token_tile(ref, t):
    return ref.at[pl.ds(pl.multiple_of(t * FFN_TM, FFN_TM), FFN_TM)]


def _dense_ffn_kernel(x_ref, wg_ref, wu_ref, wd_ref, res_ref, o_ref, res_buf, sem, *, n_f):
    t = pl.program_id(0)
    f = pl.program_id(1)
    res_copy = pltpu.make_async_copy(_token_tile(res_ref, t), res_buf, sem)

    @pl.when(f == 0)
    def _():
        res_copy.start()
        o_ref[...] = _swiglu_rows(x_ref[...], wg_ref, wu_ref, wd_ref)

    @pl.when(f > 0)
    def _():
        o_ref[...] += _swiglu_rows(x_ref[...], wg_ref, wu_ref, wd_ref)

    @pl.when(f == n_f - 1)
    def _():
        res_copy.wait()
        o_ref[...] += res_buf[...]


def _dense_ffn(h, wg, wu, wd, res):
    rows = h.shape[0]
    n_f = wg.shape[-1] // FFN_TF
    return pl.pallas_call(
        functools.partial(_dense_ffn_kernel, n_f=n_f),
        grid=(rows // FFN_TM, n_f),
        in_specs=[
            pl.BlockSpec((FFN_TM, D_MODEL), lambda t, f: (t, 0)),
            pl.BlockSpec((1, D_MODEL, FFN_TF), lambda t, f: (0, 0, f)),
            pl.BlockSpec((1, D_MODEL, FFN_TF), lambda t, f: (0, 0, f)),
            pl.BlockSpec((1, FFN_TF, D_MODEL), lambda t, f: (0, f, 0)),
            pl.BlockSpec(memory_space=pl.ANY),
        ],
        out_specs=pl.BlockSpec((FFN_TM, D_MODEL), lambda t, f: (t, 0)),
        out_shape=jax.ShapeDtypeStruct((rows, D_MODEL), F32),
        scratch_shapes=[pltpu.VMEM((FFN_TM, D_MODEL), F32), pltpu.SemaphoreType.DMA(())],
        compiler_params=_params(("arbitrary", "arbitrary")),
        name="dense_ffn",
    )(h, wg, wu, wd, res)


def _moe_ffn_kernel(te_ref, nt_ref, tv_ref, x_ref, wg_ref, wu_ref, wd_ref, o_ref, stage_ref, xb_ref, sem):
    t = pl.program_id(0)
    f = pl.program_id(1)
    live = t < nt_ref[0]
    valid = tv_ref[t]

    def x_copy(tile):
        return pltpu.make_async_copy(_token_tile(x_ref, tile), stage_ref, sem)

    @pl.when(f == 0)
    def _():
        o_ref[...] = jnp.zeros(o_ref.shape, F32)

        @pl.when(t == 0)
        def _():
            x_copy(t).start()

        @pl.when(live)
        def _():
            x_copy(t).wait()
            xb_ref[...] = stage_ref[...].astype(BF16)

    @pl.when((f == 1) & (t + 1 < nt_ref[0]))
    def _():
        x_copy(t + 1).start()

    for rows, applies in ((slice(0, FFN_TM), valid > FFN_TM // 2), (slice(0, FFN_TM // 2), valid <= FFN_TM // 2)):
        @pl.when(live & applies)
        def _():
            o_ref[rows, :] += _swiglu_rows(xb_ref[rows, :], wg_ref, wu_ref, wd_ref)


def _moe_ffn(tile_expert, n_tiles, tile_valid, x_sorted, wg, wu, wd):
    rows = x_sorted.shape[0]
    n_f = wg.shape[-1] // FFN_TF

    def hidden(t, f, nt):
        return jnp.where(t < nt[0], f, n_f - 1)

    grid_spec = pltpu.PrefetchScalarGridSpec(
        num_scalar_prefetch=3,
        grid=(rows // FFN_TM, n_f),
        in_specs=[
            pl.BlockSpec(memory_space=pl.ANY),
            pl.BlockSpec((1, D_MODEL, FFN_TF), lambda t, f, te, nt, tv: (te[t], 0, hidden(t, f, nt))),
            pl.BlockSpec((1, D_MODEL, FFN_TF), lambda t, f, te, nt, tv: (te[t], 0, hidden(t, f, nt))),
            pl.BlockSpec((1, FFN_TF, D_MODEL), lambda t, f, te, nt, tv: (te[t], hidden(t, f, nt), 0)),
        ],
        out_specs=pl.BlockSpec((FFN_TM, D_MODEL), lambda t, f, te, nt, tv: (t, 0)),
        scratch_shapes=[pltpu.VMEM((FFN_TM, D_MODEL), F32), pltpu.VMEM((FFN_TM, D_MODEL), BF16),
                        pltpu.SemaphoreType.DMA(())],
    )
    return pl.pallas_call(
        _moe_ffn_kernel,
        grid_spec=grid_spec,
        out_shape=jax.ShapeDtypeStruct((rows, D_MODEL), F32),
        compiler_params=_params(("arbitrary", "arbitrary")),
        name="moe_ffn",
    )(tile_expert, n_tiles, tile_valid, x_sorted, wg, wu, wd)


def _moe_layer(x, h, route, counts, wg, wu, wd):
    n = x.shape[0]
    n_t = (n * TOP_K) // FFN_TM + N_EXPERTS
    experts = jnp.arange(N_EXPERTS, dtype=jnp.int32)
    counts = counts[0, :N_EXPERTS].astype(jnp.int32)
    tiles_per = (counts + FFN_TM - 1) // FFN_TM
    tiles_end = jnp.cumsum(tiles_per)
    first_tile = tiles_end - tiles_per
    idx = route[:, ROUTE_EXPERT:ROUTE_EXPERT + TOP_K].astype(jnp.int32)
    rank = route[:, ROUTE_RANK:ROUTE_RANK + TOP_K].astype(jnp.int32)
    group_start = jnp.sum(jnp.where(idx[:, :, None] == experts, first_tile * FFN_TM, 0), axis=-1)
    slot = (group_start + rank).reshape(-1)

    n_tiles = tiles_end[-1:]
    t_ids = jnp.minimum(jnp.arange(n_t, dtype=jnp.int32), n_tiles[0] - 1)
    tile_expert = jnp.minimum(jnp.sum((t_ids[:, None] >= tiles_end[None, :]).astype(jnp.int32), axis=1),
                              N_EXPERTS - 1)
    of_tile = tile_expert[:, None] == experts[None, :]
    tile_valid = jnp.clip(jnp.sum(jnp.where(of_tile, counts - (t_ids[:, None] - first_tile) * FFN_TM, 0), axis=1),
                          0, FFN_TM)
    pad_start = first_tile * FFN_TM + counts
    pad_len = tiles_per * FFN_TM - counts

    x_sorted = _dispatch(slot, pad_start, pad_len, n_tiles, h, n_t * FFN_TM)
    y = _moe_ffn(tile_expert, n_tiles, tile_valid, x_sorted, wg, wu, wd)
    return _combine(slot, x, route, y)


def kernel(x, positions, attn_norm_g, w_in, q_norm_g, k_norm_g, lambda_vecs, attn_out_norm_g, w_pool, pool_scale,
           conv_w, w_out, ffn_norm_g, dense_w_gate, dense_w_up, dense_w_down, router_w, moe_w_gate, moe_w_up,
           moe_w_down):
    batch, seq, d_model = x.shape
    depth = w_in.shape[0]
    n = batch * seq

    inv_freq = 1.0 / (ROPE_THETA ** (jnp.arange(0, DIFF_HEAD_DIM, 2, dtype=F32) / DIFF_HEAD_DIM))
    ang = positions.astype(F32)[..., None] * inv_freq
    cos = jnp.cos(ang).astype(x.dtype).reshape(n, DIFF_HEAD_DIM // 2)
    sin = jnp.sin(ang).astype(x.dtype).reshape(n, DIFF_HEAD_DIM // 2)
    cos2 = jnp.concatenate([cos, cos], axis=-1)
    sin2 = jnp.concatenate([-sin, sin], axis=-1)

    xf = x.reshape(n, d_model)
    for l in range(depth):
        lam_init = 0.8 - 0.6 * math.exp(-0.3 * l)
        qk_g = jnp.stack([q_norm_g[l], k_norm_g[l]])[:, None, :]
        z_qk, z_rest = _in_proj(xf, attn_norm_g[l][None, :], w_in[l].astype(BF16), cos2, sin2, qk_g)
        attn = _attention(z_qk, z_rest, lambda_vecs[l], attn_out_norm_g[l][None, :], lam_init, batch, seq)
        mix = _mixers(z_rest, w_pool[l].astype(BF16), pool_scale[l][None, :], conv_w[l], batch, seq)
        j = l // 2
        if l % 2 == 0:
            xf, h = _out_proj(attn, mix, xf, w_out[l].astype(BF16), ffn_norm_g[l][None, :])
            xf = _dense_ffn(h, dense_w_gate[j][None], dense_w_up[j][None], dense_w_down[j][None], xf)
        else:
            rw = jnp.pad(router_w[j], ((0, 0), (0, LANES - N_EXPERTS)))
            xf, h, route, counts = _out_proj(attn, mix, xf, w_out[l].astype(BF16), ffn_norm_g[l][None, :], rw)
            xf = _moe_layer(xf, h, route, counts, moe_w_gate[j], moe_w_up[j], moe_w_down[j])
    return xf.reshape(batch, seq, d_model)
```

```python
import functools
import math

import jax
import jax.numpy as jnp
from jax import lax
from jax.experimental import pallas as pl
from jax.experimental.pallas import tpu as pltpu

D_MODEL = 2048
DIFF_HEAD_DIM = 128
DIFF_HEADS = 4
HEAD_WIDTH = 2 * DIFF_HEAD_DIM
QK_WIDTH = DIFF_HEADS * HEAD_WIDTH
ATTN_WIDTH = QK_WIDTH
POOL_WINDOWS = (2, 4, 8, 16)
POOL_GROUP = 128
POOL_WIDTH = 512
CONV_WIDTH = 512
CONV_TAPS = 3
IN_WIDTH = 5120
ROPE_THETA = 10000.0
N_EXPERTS = 8
TOP_K = 2
EPS = 1e-6
NEG_INF = -1e30
LANES = 128
Q_SCALE = DIFF_HEAD_DIM ** -0.5 * math.log2(math.e)

BF16 = jnp.bfloat16
F32 = jnp.float32

VMEM_LIMIT_BYTES = 56 * 1024 * 1024

IN_TM = 1024
IN_TN = 1024
ATTN_TQ = 512
OUT_TM = 512
FFN_TM = 1024
FFN_TF = 256
PERM_TM = 512


def _params(semantics):
    return pltpu.CompilerParams(dimension_semantics=semantics, vmem_limit_bytes=VMEM_LIMIT_BYTES)


def _in_proj_kernel(x_ref, g_ref, w_ref, cos_ref, sin_ref, qkg_ref, zqk_ref, zrest_ref, h_ref, rawq_ref, rawk_ref):
    j = pl.program_id(1)

    def project():
        return jnp.dot(h_ref[...], w_ref[...], preferred_element_type=F32)

    def head_norm_rotary(raw_ref, which):
        gain = qkg_ref[which]
        cos = cos_ref[...]
        sin = sin_ref[...]
        for c in range(IN_TN // DIFF_HEAD_DIM):
            cols = slice(c * DIFF_HEAD_DIM, (c + 1) * DIFF_HEAD_DIM)
            zc = raw_ref[:, cols]
            ms = jnp.mean(zc * zc, axis=-1, keepdims=True)
            zn = zc * lax.rsqrt(ms + EPS) * gain
            rot = zn * cos + pltpu.roll(zn, DIFF_HEAD_DIM // 2, axis=1) * sin
            if which == 0:
                rot = rot * Q_SCALE
            zqk_ref[:, cols] = rot.astype(BF16)

    @pl.when(j == 0)
    def _():
        x = x_ref[...]
        ms = jnp.mean(x * x, axis=-1, keepdims=True)
        h_ref[...] = (x * lax.rsqrt(ms + EPS) * g_ref[...]).astype(BF16)
        rawq_ref[...] = project()

    @pl.when(j == 1)
    def _():
        rawk_ref[...] = project()
        head_norm_rotary(rawq_ref, 0)

    @pl.when(j == 2)
    def _():
        zrest_ref[...] = project().astype(BF16)
        head_norm_rotary(rawk_ref, 1)

    @pl.when(j > 2)
    def _():
        zrest_ref[...] = project().astype(BF16)


def _in_proj(x, norm_g, w_in, cos2, sin2, qk_g):
    n = x.shape[0]
    n_qk = 2 * QK_WIDTH // IN_TN
    grid = (n // IN_TM, IN_WIDTH // IN_TN)
    return pl.pallas_call(
        _in_proj_kernel,
        grid=grid,
        in_specs=[
            pl.BlockSpec((IN_TM, D_MODEL), lambda i, j: (i, 0)),
            pl.BlockSpec((1, D_MODEL), lambda i, j: (0, 0)),
            pl.BlockSpec((D_MODEL, IN_TN), lambda i, j: (0, j)),
            pl.BlockSpec((IN_TM, DIFF_HEAD_DIM), lambda i, j: (i, 0)),
            pl.BlockSpec((IN_TM, DIFF_HEAD_DIM), lambda i, j: (i, 0)),
            pl.BlockSpec((2, 1, DIFF_HEAD_DIM), lambda i, j: (0, 0, 0)),
        ],
        out_specs=[
            pl.BlockSpec((IN_TM, IN_TN), lambda i, j: (i, jnp.clip(j - 1, 0, n_qk - 1))),
            pl.BlockSpec((IN_TM, IN_TN), lambda i, j: (i, jnp.maximum(j - n_qk, 0))),
        ],
        out_shape=[jax.ShapeDtypeStruct((n, 2 * QK_WIDTH), BF16),
                   jax.ShapeDtypeStruct((n, IN_WIDTH - 2 * QK_WIDTH), BF16)],
        scratch_shapes=[pltpu.VMEM((IN_TM, D_MODEL), BF16), pltpu.VMEM((IN_TM, IN_TN), F32),
                        pltpu.VMEM((IN_TM, IN_TN), F32)],
        compiler_params=_params(("parallel", "arbitrary")),
        name="in_proj",
    )(x, norm_g, w_in, cos2, sin2, qk_g)


def _attn_kernel(q_ref, k_ref, v_ref, lv_ref, subg_ref, o_ref, *, lam_init):
    t, d = ATTN_TQ, DIFF_HEAD_DIM
    seq = q_ref.shape[0]
    lv = lv_ref[...]
    lam = (jnp.exp(jnp.sum(lv[0:1] * lv[1:2], axis=-1, keepdims=True))
           - jnp.exp(jnp.sum(lv[2:3] * lv[3:4], axis=-1, keepdims=True)) + lam_init)
    causal = (lax.broadcasted_iota(jnp.int32, (t, t), 1) <= lax.broadcasted_iota(jnp.int32, (t, t), 0))
    nt_dims = (((1,), (1,)), ((), ()))

    for i in range(seq // t):
        lo = i * t
        a_diag, a_past = None, None
        for mi in range(2):
            cols = slice(mi * d, (mi + 1) * d)
            qm = q_ref[lo:lo + t, cols]
            s_diag = lax.dot_general(qm, k_ref[lo:lo + t, cols], nt_dims, preferred_element_type=F32)
            s_diag = jnp.where(causal, s_diag, NEG_INF)
            mx = jnp.max(s_diag, axis=-1, keepdims=True)
            if i > 0:
                s_past = lax.dot_general(qm, k_ref[0:lo, cols], nt_dims, preferred_element_type=F32)
                mx = jnp.maximum(mx, jnp.max(s_past, axis=-1, keepdims=True))
            p_diag = jnp.exp2(s_diag - mx)
            denom = jnp.sum(p_diag, axis=-1, keepdims=True)
            if i > 0:
                p_past = jnp.exp2(s_past - mx)
                denom = denom + jnp.sum(p_past, axis=-1, keepdims=True)
            weight = 1.0 / denom if mi == 0 else lam / denom
            if mi == 0:
                a_diag = p_diag * weight
                a_past = p_past * weight if i > 0 else None
            else:
                a_diag = a_diag - p_diag * weight
                a_past = a_past - p_past * weight if i > 0 else None
        o = jnp.dot(a_diag.astype(BF16), v_ref[lo:lo + t, :], preferred_element_type=F32)
        if i > 0:
            o = o + jnp.dot(a_past.astype(BF16), v_ref[0:lo, :], preferred_element_type=F32)
        ms = jnp.mean(o * o, axis=-1, keepdims=True)
        o_ref[lo:lo + t, :] = ((o * lax.rsqrt(ms + EPS)) * subg_ref[...] * (1.0 - lam_init)).astype(BF16)


def _attention(z_qk, z_rest, lam_vecs, sub_g, lam_init, batch, seq):
    k_col0 = QK_WIDTH // HEAD_WIDTH
    return pl.pallas_call(
        functools.partial(_attn_kernel, lam_init=lam_init),
        grid=(batch, DIFF_HEADS),
        in_specs=[
            pl.BlockSpec((seq, HEAD_WIDTH), lambda b, h: (b, h)),
            pl.BlockSpec((seq, HEAD_WIDTH), lambda b, h: (b, k_col0 + h)),
            pl.BlockSpec((seq, HEAD_WIDTH), lambda b, h: (b, h)),
            pl.BlockSpec((4, DIFF_HEAD_DIM), lambda b, h: (0, 0)),
            pl.BlockSpec((1, HEAD_WIDTH), lambda b, h: (0, 0)),
        ],
        out_specs=pl.BlockSpec((seq, HEAD_WIDTH), lambda b, h: (b, h)),
        out_shape=jax.ShapeDtypeStruct((batch * seq, ATTN_WIDTH), BF16),
        compiler_params=_params(("parallel", "parallel")),
        name="diff_attention",
    )(z_qk, z_qk, z_rest, lam_vecs, sub_g)


HALO = 16


def _mix_tile(pos0, up_ref, gb_ref, gc_ref, uc_ref, up_halo_ref, gc_halo_ref, uc_halo_ref, wp_ref, ps_ref, cw_ref,
              mix_ref):
    rows = up_ref.shape[0]
    pos = pos0 - HALO + lax.broadcasted_iota(jnp.int32, (HALO + rows, LANES), 0)

    def extended(halo_ref, ref, cols):
        x = jnp.concatenate([halo_ref[:, cols], ref[:, cols]], axis=0).astype(F32)
        return jnp.where(pos >= 0, x, 0.0)

    def delayed(x, k):
        return pltpu.roll(x, k, axis=0)

    for gi, w in enumerate(POOL_WINDOWS):
        cols = slice(gi * POOL_GROUP, (gi + 1) * POOL_GROUP)
        g = extended(up_halo_ref, up_ref, cols)
        s, span = g, 1
        while span < w:
            s = s + delayed(s, span)
            span *= 2
        count = jnp.clip(pos + 1, 1, w).astype(F32)
        pooled = (s / count - g)[HALO:]
        y = jnp.dot(pooled.astype(BF16), wp_ref[gi], preferred_element_type=F32) * ps_ref[:, cols]
        mix_ref[:, cols] = y.astype(BF16)

    for c in range(CONV_WIDTH // LANES):
        cols = slice(c * LANES, (c + 1) * LANES)
        u = extended(gc_halo_ref, gc_ref, cols) * extended(uc_halo_ref, uc_ref, cols)
        y = (cw_ref[0:1, cols] * delayed(u, 2) + cw_ref[1:2, cols] * delayed(u, 1)) + cw_ref[2:3, cols] * u
        out = gb_ref[:, cols].astype(F32) * y[HALO:]
        mix_ref[:, POOL_WIDTH + c * LANES:POOL_WIDTH + (c + 1) * LANES] = out.astype(BF16)


ROUTE_GATE, ROUTE_EXPERT, ROUTE_RANK = 0, 2, 4


N_MIX_REFS = 10


def _out_proj_kernel(*refs, with_router, tiles_per_seq):
    attn_ref, mix_in = refs[0], refs[1:1 + N_MIX_REFS]
    x_ref, wo_ref, g_ref, *rest = refs[1 + N_MIX_REFS:]
    if with_router:
        rw_ref, xo_ref, h_ref, route_ref, counts_ref, mix_ref, run_ref, r2_ref = rest
    else:
        xo_ref, h_ref, mix_ref = rest
    y = jnp.dot(attn_ref[...], wo_ref[:ATTN_WIDTH, :], preferred_element_type=F32)
    _mix_tile((pl.program_id(0) % tiles_per_seq) * OUT_TM, *mix_in, mix_ref)
    y = y + jnp.dot(mix_ref[...], wo_ref[ATTN_WIDTH:, :], preferred_element_type=F32)
    xn = x_ref[...] + y
    xo_ref[...] = xn
    ms = jnp.mean(xn * xn, axis=-1, keepdims=True)
    hf = xn * lax.rsqrt(ms + EPS) * g_ref[...]
    h_ref[...] = hf.astype(h_ref.dtype)
    if with_router:
        @pl.when(pl.program_id(0) == 0)
        def _():
            rw = rw_ref[...]
            r_hi = rw.astype(BF16)
            r2_ref[:, :LANES] = r_hi
            r2_ref[:, LANES:] = (rw - r_hi.astype(F32)).astype(BF16)
            run_ref[...] = jnp.zeros(run_ref.shape, F32)

        h_hi = hf.astype(BF16)
        h_lo = (hf - h_hi.astype(F32)).astype(BF16)
        hi_terms = jnp.dot(h_hi, r2_ref[...], preferred_element_type=F32)
        logits = (hi_terms[:, :LANES] + hi_terms[:, LANES:]
                  + jnp.dot(h_lo, r2_ref[:, :LANES], preferred_element_type=F32))
        tm = logits.shape[0]
        lane = lax.broadcasted_iota(jnp.int32, logits.shape, 1)
        logits = jnp.where(lane < N_EXPERTS, logits, -jnp.inf)
        v1 = jnp.max(logits, axis=-1, keepdims=True)
        i1 = jnp.min(jnp.where(logits == v1, lane, LANES), axis=-1, keepdims=True)
        rest = jnp.where(lane == i1, -jnp.inf, logits)
        v2 = jnp.max(rest, axis=-1, keepdims=True)
        i2 = jnp.min(jnp.where(rest == v2, lane, LANES), axis=-1, keepdims=True)
        e2 = jnp.exp(v2 - v1)
        g1 = 1.0 / (1.0 + e2)
        g2 = e2 / (1.0 + e2)

        sel1 = lane == i1
        sel2 = lane == i2
        chosen = jnp.where(sel1 | sel2, 1.0, 0.0)
        earlier = (lax.broadcasted_iota(jnp.int32, (tm, tm), 1) < lax.broadcasted_iota(jnp.int32, (tm, tm), 0))
        before = jnp.dot(earlier.astype(BF16), chosen.astype(BF16), preferred_element_type=F32) + run_ref[...]
        r1 = jnp.sum(jnp.where(sel1, before, 0.0), axis=-1, keepdims=True)
        r2 = jnp.sum(jnp.where(sel2, before, 0.0), axis=-1, keepdims=True)
        run_ref[...] += jnp.sum(chosen, axis=0, keepdims=True)
        counts_ref[...] = jnp.broadcast_to(run_ref[...], counts_ref.shape)

        fields = (g1, g2, i1.astype(F32), i2.astype(F32), r1, r2)
        route = jnp.zeros(logits.shape, F32)
        for k, val in enumerate(fields):
            route = jnp.where(lane == k, val, route)
        route_ref[...] = route


def _out_proj(attn, z_rest, w_pool, pool_scale, conv_w, x, w_out, ffn_g, seq, router_w=None):
    n = x.shape[0]
    with_router = router_w is not None
    row_spec = lambda width: pl.BlockSpec((OUT_TM, width), lambda i: (i, 0))
    whole = lambda shape: pl.BlockSpec(shape, lambda i: (0,) * len(shape))
    col0 = ATTN_WIDTH // POOL_WIDTH
    z_tile = lambda off: pl.BlockSpec((OUT_TM, POOL_WIDTH), lambda i: (i, col0 + off))
    z_halo = lambda off: pl.BlockSpec((HALO, POOL_WIDTH),
                                      lambda i: (jnp.maximum(i * (OUT_TM // HALO) - 1, 0), col0 + off))
    in_specs = [row_spec(ATTN_WIDTH), z_tile(0), z_tile(1), z_tile(2), z_tile(3), z_halo(0), z_halo(2), z_halo(3),
                whole((len(POOL_WINDOWS), POOL_GROUP, POOL_GROUP)), whole((1, POOL_WIDTH)),
                whole((CONV_TAPS, CONV_WIDTH)),
                row_spec(D_MODEL), whole((D_MODEL, D_MODEL)), whole((1, D_MODEL))]
    out_specs = [row_spec(D_MODEL), row_spec(D_MODEL)]
    out_shape = [jax.ShapeDtypeStruct((n, D_MODEL), F32),
                 jax.ShapeDtypeStruct((n, D_MODEL), F32 if with_router else BF16)]
    args = [attn] + [z_rest] * 7 + [w_pool, pool_scale, conv_w, x, w_out, ffn_g]
    scratch = [pltpu.VMEM((OUT_TM, POOL_WIDTH + CONV_WIDTH), BF16)]
    if with_router:
        in_specs.append(whole((D_MODEL, LANES)))
        out_specs += [row_spec(LANES), whole((8, LANES))]
        out_shape += [jax.ShapeDtypeStruct((n, LANES), F32), jax.ShapeDtypeStruct((8, LANES), F32)]
        args.append(router_w)
        scratch += [pltpu.VMEM((1, LANES), F32), pltpu.VMEM((D_MODEL, 2 * LANES), BF16)]
    return pl.pallas_call(
        functools.partial(_out_proj_kernel, with_router=with_router, tiles_per_seq=seq // OUT_TM),
        grid=(n // OUT_TM,),
        in_specs=in_specs,
        out_specs=out_specs,
        out_shape=out_shape,
        scratch_shapes=scratch,
        compiler_params=_params(("arbitrary",) if with_router else ("parallel",)),
        name="out_proj_router" if with_router else "out_proj",
    )(*args)


def _wait_rows(src_rows, dst_rows, sem):
    pltpu.make_async_copy(src_rows, dst_rows, sem).wait()


SUBLANES = 8
PAD_CHUNK_BITS = (FFN_TM // SUBLANES // 2).bit_length()


def _pad_fill_copies(pad_start_ref, pad_len_ref, nt_ref, zero_ref, xs_ref, sem, act):
    half = zero_ref.shape[0]

    def dead_tile(t, carry):
        for part in range(FFN_TM // half):
            rows = pl.ds(pl.multiple_of(t * FFN_TM + part * half, half), half)
            act(pltpu.make_async_copy(zero_ref, xs_ref.at[rows], sem))
        return carry

    lax.fori_loop(nt_ref[0], xs_ref.shape[0] // FFN_TM, dead_tile, 0)
    for e in range(N_EXPERTS):
        start, length = pad_start_ref[e], pad_len_ref[e]
        head = jnp.minimum((SUBLANES - start % SUBLANES) % SUBLANES, length)
        for i in range(SUBLANES - 1):
            @pl.when(i < head)
            def _():
                act(pltpu.make_async_copy(zero_ref.at[pl.ds(0, 1)], xs_ref.at[pl.ds(start + i, 1)], sem))

        aligned = start + head
        groups = (length - head) // SUBLANES
        for b in reversed(range(PAD_CHUNK_BITS)):
            size = SUBLANES << b

            @pl.when(((groups >> b) & 1) == 1)
            def _():
                covered = ((groups >> (b + 1)) << (b + 1)) * SUBLANES
                rows = pl.ds(pl.multiple_of(aligned + covered, SUBLANES), size)
                act(pltpu.make_async_copy(zero_ref.at[pl.ds(0, size)], xs_ref.at[rows], sem))


def _dispatch_kernel(slot_ref, pad_start_ref, pad_len_ref, nt_ref, h_ref, xs_ref, zero_ref, sem, pad_sem):
    rows = h_ref.shape[0]
    base = pl.program_id(0) * rows * TOP_K
    pad_args = (pad_start_ref, pad_len_ref, nt_ref, zero_ref, xs_ref, pad_sem)

    @pl.when(pl.program_id(0) == 0)
    def _():
        zero_ref[...] = jnp.zeros(zero_ref.shape, F32)
        _pad_fill_copies(*pad_args, lambda copy: copy.start())

    def issue(r, carry):
        for k in range(TOP_K):
            s = slot_ref[base + r * TOP_K + k]
            pltpu.make_async_copy(h_ref.at[pl.ds(r, 1)], xs_ref.at[pl.ds(s, 1)], sem).start()
        return carry

    lax.fori_loop(0, rows, issue, 0, unroll=8)
    for _ in range(TOP_K):
        _wait_rows(h_ref, xs_ref.at[pl.ds(0, rows)], sem)

    @pl.when(pl.program_id(0) == 0)
    def _():
        _pad_fill_copies(*pad_args, lambda copy: copy.wait())


def _dispatch(slot, pad_start, pad_len, n_tiles, h, n_slots):
    n = h.shape[0]
    grid_spec = pltpu.PrefetchScalarGridSpec(
        num_scalar_prefetch=4,
        grid=(n // PERM_TM,),
        in_specs=[pl.BlockSpec((PERM_TM, D_MODEL), lambda i, s, ps, pn, nt: (i, 0))],
        out_specs=pl.BlockSpec(memory_space=pl.ANY),
        scratch_shapes=[pltpu.VMEM((FFN_TM // 2, D_MODEL), F32), pltpu.SemaphoreType.DMA(()),
                        pltpu.SemaphoreType.DMA(())],
    )
    return pl.pallas_call(
        _dispatch_kernel,
        grid_spec=grid_spec,
        out_shape=jax.ShapeDtypeStruct((n_slots, D_MODEL), F32),
        compiler_params=_params(("arbitrary",)),
        name="moe_dispatch",
    )(slot, pad_start, pad_len, n_tiles, h)


def _combine_kernel(slot_ref, x_ref, route_ref, y_ref, o_ref, rows_ref, sems):
    rows = x_ref.shape[0]
    step = pl.program_id(0)

    def start_gather(of_step):
        buf = of_step % 2
        base = of_step * rows * TOP_K

        def issue(r, carry):
            for k in range(TOP_K):
                s = slot_ref[base + r * TOP_K + k]
                pltpu.make_async_copy(y_ref.at[pl.ds(s, 1)], rows_ref.at[buf, k, pl.ds(r, 1)], sems.at[buf]).start()
            return carry

        lax.fori_loop(0, rows, issue, 0, unroll=8)

    @pl.when(step == 0)
    def _():
        start_gather(step)

    @pl.when(step + 1 < pl.num_programs(0))
    def _():
        start_gather(step + 1)

    buf = step % 2
    for k in range(TOP_K):
        _wait_rows(y_ref.at[pl.ds(0, rows)], rows_ref.at[buf, k], sems.at[buf])
    route = route_ref[...]
    out = x_ref[...]
    for k in range(TOP_K):
        out = out + route[:, ROUTE_GATE + k:ROUTE_GATE + k + 1] * rows_ref[buf, k]
    o_ref[...] = out


def _combine(slot, x, route, y):
    n = x.shape[0]
    grid_spec = pltpu.PrefetchScalarGridSpec(
        num_scalar_prefetch=1,
        grid=(n // PERM_TM,),
        in_specs=[
            pl.BlockSpec((PERM_TM, D_MODEL), lambda i, s: (i, 0)),
            pl.BlockSpec((PERM_TM, LANES), lambda i, s: (i, 0)),
            pl.BlockSpec(memory_space=pl.ANY),
        ],
        out_specs=pl.BlockSpec((PERM_TM, D_MODEL), lambda i, s: (i, 0)),
        scratch_shapes=[pltpu.VMEM((2, TOP_K, PERM_TM, D_MODEL), F32), pltpu.SemaphoreType.DMA((2,))],
    )
    return pl.pallas_call(
        _combine_kernel,
        grid_spec=grid_spec,
        out_shape=jax.ShapeDtypeStruct((n, D_MODEL), F32),
        compiler_params=_params(("arbitrary",)),
        name="moe_combine",
    )(slot, x, route, y)


def _swiglu_rows(x, wg_ref, wu_ref, wd_ref):
    g = jnp.dot(x, wg_ref[0].astype(BF16), preferred_element_type=F32)
    u = jnp.dot(x, wu_ref[0].astype(BF16), preferred_element_type=F32)
    a = (g * jax.nn.sigmoid(g) * u).astype(BF16)
    return jnp.dot(a, wd_ref[0].astype(BF16), preferred_element_type=F32)


def _token_tile(ref, t):
    return ref.at[pl.ds(pl.multiple_of(t * FFN_TM, FFN_TM), FFN_TM)]


def _dense_ffn_kernel(x_ref, wg_ref, wu_ref, wd_ref, res_ref, o_ref, res_buf, sem, *, n_f):
    t = pl.program_id(0)
    f = pl.program_id(1)
    res_copy = pltpu.make_async_copy(_token_tile(res_ref, t), res_buf, sem)

    @pl.when(f == 0)
    def _():
        res_copy.start()
        o_ref[...] = _swiglu_rows(x_ref[...], wg_ref, wu_ref, wd_ref)

    @pl.when(f > 0)
    def _():
        o_ref[...] += _swiglu_rows(x_ref[...], wg_ref, wu_ref, wd_ref)

    @pl.when(f == n_f - 1)
    def _():
        res_copy.wait()
        o_ref[...] += res_buf[...]


def _dense_ffn(h, wg, wu, wd, res):
    rows = h.shape[0]
    n_f = wg.shape[-1] // FFN_TF
    return pl.pallas_call(
        functools.partial(_dense_ffn_kernel, n_f=n_f),
        grid=(rows // FFN_TM, n_f),
        in_specs=[
            pl.BlockSpec((FFN_TM, D_MODEL), lambda t, f: (t, 0)),
            pl.BlockSpec((1, D_MODEL, FFN_TF), lambda t, f: (0, 0, f)),
            pl.BlockSpec((1, D_MODEL, FFN_TF), lambda t, f: (0, 0, f)),
            pl.BlockSpec((1, FFN_TF, D_MODEL), lambda t, f: (0, f, 0)),
            pl.BlockSpec(memory_space=pl.ANY),
        ],
        out_specs=pl.BlockSpec((FFN_TM, D_MODEL), lambda t, f: (t, 0)),
        out_shape=jax.ShapeDtypeStruct((rows, D_MODEL), F32),
        scratch_shapes=[pltpu.VMEM((FFN_TM, D_MODEL), F32), pltpu.SemaphoreType.DMA(())],
        compiler_params=_params(("arbitrary", "arbitrary")),
        name="dense_ffn",
    )(h, wg, wu, wd, res)


def _moe_ffn_kernel(te_ref, nt_ref, tv_ref, x_ref, wg_ref, wu_ref, wd_ref, o_ref, stage_ref, xb_ref, sem):
    t = pl.program_id(0)
    f = pl.program_id(1)
    live = t < nt_ref[0]
    valid = tv_ref[t]

    def x_copy(tile):
        return pltpu.make_async_copy(_token_tile(x_ref, tile), stage_ref, sem)

    @pl.when(f == 0)
    def _():
        o_ref[...] = jnp.zeros(o_ref.shape, F32)

        @pl.when(t == 0)
        def _():
            x_copy(t).start()

        @pl.when(live)
        def _():
            x_copy(t).wait()
            xb_ref[...] = stage_ref[...].astype(BF16)

    @pl.when((f == 1) & (t + 1 < nt_ref[0]))
    def _():
        x_copy(t + 1).start()

    for rows, applies in ((slice(0, FFN_TM), valid > FFN_TM // 2), (slice(0, FFN_TM // 2), valid <= FFN_TM // 2)):
        @pl.when(live & applies)
        def _():
            o_ref[rows, :] += _swiglu_rows(xb_ref[rows, :], wg_ref, wu_ref, wd_ref)


def _moe_ffn(tile_expert, n_tiles, tile_valid, x_sorted, wg, wu, wd):
    rows = x_sorted.shape[0]
    n_f = wg.shape[-1] // FFN_TF

    def hidden(t, f, nt):
        return jnp.where(t < nt[0], f, n_f - 1)

    grid_spec = pltpu.PrefetchScalarGridSpec(
        num_scalar_prefetch=3,
        grid=(rows // FFN_TM, n_f),
        in_specs=[
            pl.BlockSpec(memory_space=pl.ANY),
            pl.BlockSpec((1, D_MODEL, FFN_TF), lambda t, f, te, nt, tv: (te[t], 0, hidden(t, f, nt))),
            pl.BlockSpec((1, D_MODEL, FFN_TF), lambda t, f, te, nt, tv: (te[t], 0, hidden(t, f, nt))),
            pl.BlockSpec((1, FFN_TF, D_MODEL), lambda t, f, te, nt, tv: (te[t], hidden(t, f, nt), 0)),
        ],
        out_specs=pl.BlockSpec((FFN_TM, D_MODEL), lambda t, f, te, nt, tv: (t, 0)),
        scratch_shapes=[pltpu.VMEM((FFN_TM, D_MODEL), F32), pltpu.VMEM((FFN_TM, D_MODEL), BF16),
                        pltpu.SemaphoreType.DMA(())],
    )
    return pl.pallas_call(
        _moe_ffn_kernel,
        grid_spec=grid_spec,
        out_shape=jax.ShapeDtypeStruct((rows, D_MODEL), F32),
        compiler_params=_params(("arbitrary", "arbitrary")),
        name="moe_ffn",
    )(tile_expert, n_tiles, tile_valid, x_sorted, wg, wu, wd)


def _moe_layer(x, h, route, counts, wg, wu, wd):
    n = x.shape[0]
    n_t = (n * TOP_K) // FFN_TM + N_EXPERTS
    experts = jnp.arange(N_EXPERTS, dtype=jnp.int32)
    counts = counts[0, :N_EXPERTS].astype(jnp.int32)
    tiles_per = (counts + FFN_TM - 1) // FFN_TM
    tiles_end = jnp.cumsum(tiles_per)
    first_tile = tiles_end - tiles_per
    idx = route[:, ROUTE_EXPERT:ROUTE_EXPERT + TOP_K].astype(jnp.int32)
    rank = route[:, ROUTE_RANK:ROUTE_RANK + TOP_K].astype(jnp.int32)
    group_start = jnp.sum(jnp.where(idx[:, :, None] == experts, first_tile * FFN_TM, 0), axis=-1)
    slot = (group_start + rank).reshape(-1)

    n_tiles = tiles_end[-1:]
    t_ids = jnp.minimum(jnp.arange(n_t, dtype=jnp.int32), n_tiles[0] - 1)
    tile_expert = jnp.minimum(jnp.sum((t_ids[:, None] >= tiles_end[None, :]).astype(jnp.int32), axis=1),
                              N_EXPERTS - 1)
    of_tile = tile_expert[:, None] == experts[None, :]
    tile_valid = jnp.clip(jnp.sum(jnp.where(of_tile, counts - (t_ids[:, None] - first_tile) * FFN_TM, 0), axis=1),
                          0, FFN_TM)
    pad_start = first_tile * FFN_TM + counts
    pad_len = tiles_per * FFN_TM - counts

    x_sorted = _dispatch(slot, pad_start, pad_len, n_tiles, h, n_t * FFN_TM)
    y = _moe_ffn(tile_expert, n_tiles, tile_valid, x_sorted, wg, wu, wd)
    return _combine(slot, x, route, y)


def kernel(x, positions, attn_norm_g, w_in, q_norm_g, k_norm_g, lambda_vecs, attn_out_norm_g, w_pool, pool_scale,
           conv_w, w_out, ffn_norm_g, dense_w_gate, dense_w_up, dense_w_down, router_w, moe_w_gate, moe_w_up,
           moe_w_down):
    batch, seq, d_model = x.shape
    depth = w_in.shape[0]
    n = batch * seq

    inv_freq = 1.0 / (ROPE_THETA ** (jnp.arange(0, DIFF_HEAD_DIM, 2, dtype=F32) / DIFF_HEAD_DIM))
    ang = positions.astype(F32)[..., None] * inv_freq
    cos = jnp.cos(ang).astype(x.dtype).reshape(n, DIFF_HEAD_DIM // 2)
    sin = jnp.sin(ang).astype(x.dtype).reshape(n, DIFF_HEAD_DIM // 2)
    cos2 = jnp.concatenate([cos, cos], axis=-1)
    sin2 = jnp.concatenate([-sin, sin], axis=-1)

    xf = x.reshape(n, d_model)
    for l in range(depth):
        lam_init = 0.8 - 0.6 * math.exp(-0.3 * l)
        qk_g = jnp.stack([q_norm_g[l], k_norm_g[l]])[:, None, :]
        z_qk, z_rest = _in_proj(xf, attn_norm_g[l][None, :], w_in[l].astype(BF16), cos2, sin2, qk_g)
        attn = _attention(z_qk, z_rest, lambda_vecs[l], attn_out_norm_g[l][None, :], lam_init, batch, seq)
        out_proj_args = (attn, z_rest, w_pool[l].astype(BF16), pool_scale[l][None, :], conv_w[l], xf,
                         w_out[l].astype(BF16), ffn_norm_g[l][None, :], seq)
        j = l // 2
        if l % 2 == 0:
            xf, h = _out_proj(*out_proj_args)
            xf = _dense_ffn(h, dense_w_gate[j][None], dense_w_up[j][None], dense_w_down[j][None], xf)
        else:
            rw = jnp.pad(router_w[j], ((0, 0), (0, LANES - N_EXPERTS)))
            xf, h, route, counts = _out_proj(*out_proj_args, router_w=rw)
            xf = _moe_layer(xf, h, route, counts, moe_w_gate[j], moe_w_up[j], moe_w_down[j])
    return xf.reshape(batch, seq, d_model)
```

```python
import functools
import math

import jax
import jax.numpy as jnp
from jax import lax
from jax.experimental import pallas as pl
from jax.experimental.pallas import tpu as pltpu

D_MODEL = 2048
DIFF_HEAD_DIM = 128
DIFF_HEADS = 4
HEAD_WIDTH = 2 * DIFF_HEAD_DIM
QK_WIDTH = DIFF_HEADS * HEAD_WIDTH
ATTN_WIDTH = QK_WIDTH
POOL_WINDOWS = (2, 4, 8, 16)
POOL_GROUP = 128
POOL_WIDTH = 512
CONV_WIDTH = 512
CONV_TAPS = 3
IN_WIDTH = 5120
ROPE_THETA = 10000.0
N_EXPERTS = 8
TOP_K = 2
EPS = 1e-6
NEG_INF = -1e30
LANES = 128
Q_SCALE = DIFF_HEAD_DIM ** -0.5 * math.log2(math.e)

BF16 = jnp.bfloat16
F32 = jnp.float32

VMEM_LIMIT_BYTES = 56 * 1024 * 1024
MOE_VMEM_LIMIT_BYTES = 60 * 1024 * 1024

IN_TM = 1024
IN_TN = 1024
ATTN_TQ = 512
OUT_TM = 512
FFN_TM = 1024
FFN_TF = 256
MOE_TF = 512
PERM_TM = 512


def _params(semantics, vmem_limit_bytes=VMEM_LIMIT_BYTES):
    return pltpu.CompilerParams(dimension_semantics=semantics, vmem_limit_bytes=vmem_limit_bytes)


def _in_proj_kernel(x_ref, g_ref, w_ref, cos_ref, sin_ref, qkg_ref, zqk_ref, zrest_ref, h_ref, rawq_ref, rawk_ref):
    j = pl.program_id(1)

    def project():
        return jnp.dot(h_ref[...], w_ref[...], preferred_element_type=F32)

    def head_norm_rotary(raw_ref, which):
        gain = qkg_ref[which]
        cos = cos_ref[...]
        sin = sin_ref[...]
        for c in range(IN_TN // DIFF_HEAD_DIM):
            cols = slice(c * DIFF_HEAD_DIM, (c + 1) * DIFF_HEAD_DIM)
            zc = raw_ref[:, cols]
            ms = jnp.mean(zc * zc, axis=-1, keepdims=True)
            zn = zc * lax.rsqrt(ms + EPS) * gain
            rot = zn * cos + pltpu.roll(zn, DIFF_HEAD_DIM // 2, axis=1) * sin
            if which == 0:
                rot = rot * Q_SCALE
            zqk_ref[:, cols] = rot.astype(BF16)

    @pl.when(j == 0)
    def _():
        x = x_ref[...]
        ms = jnp.mean(x * x, axis=-1, keepdims=True)
        h_ref[...] = (x * lax.rsqrt(ms + EPS) * g_ref[...]).astype(BF16)
        rawq_ref[...] = project()

    @pl.when(j == 1)
    def _():
        rawk_ref[...] = project()
        head_norm_rotary(rawq_ref, 0)

    @pl.when(j == 2)
    def _():
        zrest_ref[...] = project().astype(BF16)
        head_norm_rotary(rawk_ref, 1)

    @pl.when(j > 2)
    def _():
        zrest_ref[...] = project().astype(BF16)


def _in_proj(x, norm_g, w_in, cos2, sin2, qk_g):
    n = x.shape[0]
    n_qk = 2 * QK_WIDTH // IN_TN
    grid = (n // IN_TM, IN_WIDTH // IN_TN)
    return pl.pallas_call(
        _in_proj_kernel,
        grid=grid,
        in_specs=[
            pl.BlockSpec((IN_TM, D_MODEL), lambda i, j: (i, 0)),
            pl.BlockSpec((1, D_MODEL), lambda i, j: (0, 0)),
            pl.BlockSpec((D_MODEL, IN_TN), lambda i, j: (0, j)),
            pl.BlockSpec((IN_TM, DIFF_HEAD_DIM), lambda i, j: (i, 0)),
            pl.BlockSpec((IN_TM, DIFF_HEAD_DIM), lambda i, j: (i, 0)),
            pl.BlockSpec((2, 1, DIFF_HEAD_DIM), lambda i, j: (0, 0, 0)),
        ],
        out_specs=[
            pl.BlockSpec((IN_TM, IN_TN), lambda i, j: (i, jnp.clip(j - 1, 0, n_qk - 1))),
            pl.BlockSpec((IN_TM, IN_TN), lambda i, j: (i, jnp.maximum(j - n_qk, 0))),
        ],
        out_shape=[jax.ShapeDtypeStruct((n, 2 * QK_WIDTH), BF16),
                   jax.ShapeDtypeStruct((n, IN_WIDTH - 2 * QK_WIDTH), BF16)],
        scratch_shapes=[pltpu.VMEM((IN_TM, D_MODEL), BF16), pltpu.VMEM((IN_TM, IN_TN), F32),
                        pltpu.VMEM((IN_TM, IN_TN), F32)],
        compiler_params=_params(("parallel", "arbitrary")),
        name="in_proj",
    )(x, norm_g, w_in, cos2, sin2, qk_g)


def _attn_kernel(q_ref, k_ref, v_ref, lv_ref, subg_ref, o_ref, *, lam_init):
    t, d = ATTN_TQ, DIFF_HEAD_DIM
    seq = q_ref.shape[0]
    lv = lv_ref[...]
    lam = (jnp.exp(jnp.sum(lv[0:1] * lv[1:2], axis=-1, keepdims=True))
           - jnp.exp(jnp.sum(lv[2:3] * lv[3:4], axis=-1, keepdims=True)) + lam_init)
    causal = (lax.broadcasted_iota(jnp.int32, (t, t), 1) <= lax.broadcasted_iota(jnp.int32, (t, t), 0))
    nt_dims = (((1,), (1,)), ((), ()))

    for i in range(seq // t):
        lo = i * t
        a_diag, a_past = None, None
        for mi in range(2):
            cols = slice(mi * d, (mi + 1) * d)
            qm = q_ref[lo:lo + t, cols]
            s_diag = lax.dot_general(qm, k_ref[lo:lo + t, cols], nt_dims, preferred_element_type=F32)
            s_diag = jnp.where(causal, s_diag, NEG_INF)
            mx = jnp.max(s_diag, axis=-1, keepdims=True)
            if i > 0:
                s_past = lax.dot_general(qm, k_ref[0:lo, cols], nt_dims, preferred_element_type=F32)
                mx = jnp.maximum(mx, jnp.max(s_past, axis=-1, keepdims=True))
            p_diag = jnp.exp2(s_diag - mx)
            denom = jnp.sum(p_diag, axis=-1, keepdims=True)
            if i > 0:
                p_past = jnp.exp2(s_past - mx)
                denom = denom + jnp.sum(p_past, axis=-1, keepdims=True)
            weight = 1.0 / denom if mi == 0 else lam / denom
            if mi == 0:
                a_diag = p_diag * weight
                a_past = p_past * weight if i > 0 else None
            else:
                a_diag = a_diag - p_diag * weight
                a_past = a_past - p_past * weight if i > 0 else None
        o = jnp.dot(a_diag.astype(BF16), v_ref[lo:lo + t, :], preferred_element_type=F32)
        if i > 0:
            o = o + jnp.dot(a_past.astype(BF16), v_ref[0:lo, :], preferred_element_type=F32)
        ms = jnp.mean(o * o, axis=-1, keepdims=True)
        o_ref[lo:lo + t, :] = ((o * lax.rsqrt(ms + EPS)) * subg_ref[...] * (1.0 - lam_init)).astype(BF16)


def _attention(z_qk, z_rest, lam_vecs, sub_g, lam_init, batch, seq):
    k_col0 = QK_WIDTH // HEAD_WIDTH
    return pl.pallas_call(
        functools.partial(_attn_kernel, lam_init=lam_init),
        grid=(batch, DIFF_HEADS),
        in_specs=[
            pl.BlockSpec((seq, HEAD_WIDTH), lambda b, h: (b, h)),
            pl.BlockSpec((seq, HEAD_WIDTH), lambda b, h: (b, k_col0 + h)),
            pl.BlockSpec((seq, HEAD_WIDTH), lambda b, h: (b, h)),
            pl.BlockSpec((4, DIFF_HEAD_DIM), lambda b, h: (0, 0)),
            pl.BlockSpec((1, HEAD_WIDTH), lambda b, h: (0, 0)),
        ],
        out_specs=pl.BlockSpec((seq, HEAD_WIDTH), lambda b, h: (b, h)),
        out_shape=jax.ShapeDtypeStruct((batch * seq, ATTN_WIDTH), BF16),
        compiler_params=_params(("parallel", "parallel")),
        name="diff_attention",
    )(z_qk, z_qk, z_rest, lam_vecs, sub_g)


HALO = 16


def _mix_tile(pos0, up_ref, gb_ref, gc_ref, uc_ref, up_halo_ref, gc_halo_ref, uc_halo_ref, wp_ref, ps_ref, cw_ref,
              mix_ref):
    rows = up_ref.shape[0]
    pos = pos0 - HALO + lax.broadcasted_iota(jnp.int32, (HALO + rows, LANES), 0)

    def extended(halo_ref, ref, cols):
        x = jnp.concatenate([halo_ref[:, cols], ref[:, cols]], axis=0).astype(F32)
        return jnp.where(pos >= 0, x, 0.0)

    def delayed(x, k):
        return pltpu.roll(x, k, axis=0)

    for gi, w in enumerate(POOL_WINDOWS):
        cols = slice(gi * POOL_GROUP, (gi + 1) * POOL_GROUP)
        g = extended(up_halo_ref, up_ref, cols)
        s, span = g, 1
        while span < w:
            s = s + delayed(s, span)
            span *= 2
        count = jnp.clip(pos + 1, 1, w).astype(F32)
        pooled = (s / count - g)[HALO:]
        y = jnp.dot(pooled.astype(BF16), wp_ref[gi], preferred_element_type=F32) * ps_ref[:, cols]
        mix_ref[:, cols] = y.astype(BF16)

    for c in range(CONV_WIDTH // LANES):
        cols = slice(c * LANES, (c + 1) * LANES)
        u = extended(gc_halo_ref, gc_ref, cols) * extended(uc_halo_ref, uc_ref, cols)
        y = (cw_ref[0:1, cols] * delayed(u, 2) + cw_ref[1:2, cols] * delayed(u, 1)) + cw_ref[2:3, cols] * u
        out = gb_ref[:, cols].astype(F32) * y[HALO:]
        mix_ref[:, POOL_WIDTH + c * LANES:POOL_WIDTH + (c + 1) * LANES] = out.astype(BF16)


ROUTE_GATE, ROUTE_EXPERT, ROUTE_RANK = 0, 2, 4


N_MIX_REFS = 10


def _out_proj_kernel(*refs, with_router, tiles_per_seq):
    attn_ref, mix_in = refs[0], refs[1:1 + N_MIX_REFS]
    x_ref, wo_ref, g_ref, *rest = refs[1 + N_MIX_REFS:]
    if with_router:
        rw_ref, xo_ref, h_ref, route_ref, counts_ref, mix_ref, run_ref, r2_ref = rest
    else:
        xo_ref, h_ref, mix_ref = rest
    y = jnp.dot(attn_ref[...], wo_ref[:ATTN_WIDTH, :], preferred_element_type=F32)
    _mix_tile((pl.program_id(0) % tiles_per_seq) * OUT_TM, *mix_in, mix_ref)
    y = y + jnp.dot(mix_ref[...], wo_ref[ATTN_WIDTH:, :], preferred_element_type=F32)
    xn = x_ref[...] + y
    xo_ref[...] = xn
    ms = jnp.mean(xn * xn, axis=-1, keepdims=True)
    hf = xn * lax.rsqrt(ms + EPS) * g_ref[...]
    h_ref[...] = hf.astype(h_ref.dtype)
    if with_router:
        @pl.when(pl.program_id(0) == 0)
        def _():
            rw = rw_ref[...]
            r_hi = rw.astype(BF16)
            r2_ref[:, :LANES] = r_hi
            r2_ref[:, LANES:] = (rw - r_hi.astype(F32)).astype(BF16)
            run_ref[...] = jnp.zeros(run_ref.shape, F32)

        h_hi = hf.astype(BF16)
        h_lo = (hf - h_hi.astype(F32)).astype(BF16)
        hi_terms = jnp.dot(h_hi, r2_ref[...], preferred_element_type=F32)
        logits = (hi_terms[:, :LANES] + hi_terms[:, LANES:]
                  + jnp.dot(h_lo, r2_ref[:, :LANES], preferred_element_type=F32))
        tm = logits.shape[0]
        lane = lax.broadcasted_iota(jnp.int32, logits.shape, 1)
        logits = jnp.where(lane < N_EXPERTS, logits, -jnp.inf)
        v1 = jnp.max(logits, axis=-1, keepdims=True)
        i1 = jnp.min(jnp.where(logits == v1, lane, LANES), axis=-1, keepdims=True)
        rest = jnp.where(lane == i1, -jnp.inf, logits)
        v2 = jnp.max(rest, axis=-1, keepdims=True)
        i2 = jnp.min(jnp.where(rest == v2, lane, LANES), axis=-1, keepdims=True)
        e2 = jnp.exp(v2 - v1)
        g1 = 1.0 / (1.0 + e2)
        g2 = e2 / (1.0 + e2)

        sel1 = lane == i1
        sel2 = lane == i2
        chosen = jnp.where(sel1 | sel2, 1.0, 0.0)
        earlier = (lax.broadcasted_iota(jnp.int32, (tm, tm), 1) < lax.broadcasted_iota(jnp.int32, (tm, tm), 0))
        before = jnp.dot(earlier.astype(BF16), chosen.astype(BF16), preferred_element_type=F32) + run_ref[...]
        r1 = jnp.sum(jnp.where(sel1, before, 0.0), axis=-1, keepdims=True)
        r2 = jnp.sum(jnp.where(sel2, before, 0.0), axis=-1, keepdims=True)
        run_ref[...] += jnp.sum(chosen, axis=0, keepdims=True)
        counts_ref[...] = jnp.broadcast_to(run_ref[...], counts_ref.shape)

        fields = (g1, g2, i1.astype(F32), i2.astype(F32), r1, r2)
        route = jnp.zeros(logits.shape, F32)
        for k, val in enumerate(fields):
            route = jnp.where(lane == k, val, route)
        route_ref[...] = route


def _out_proj(attn, z_rest, w_pool, pool_scale, conv_w, x, w_out, ffn_g, seq, router_w=None):
    n = x.shape[0]
    with_router = router_w is not None
    row_spec = lambda width: pl.BlockSpec((OUT_TM, width), lambda i: (i, 0))
    whole = lambda shape: pl.BlockSpec(shape, lambda i: (0,) * len(shape))
    col0 = ATTN_WIDTH // POOL_WIDTH
    z_tile = lambda off: pl.BlockSpec((OUT_TM, POOL_WIDTH), lambda i: (i, col0 + off))
    z_halo = lambda off: pl.BlockSpec((HALO, POOL_WIDTH),
                                      lambda i: (jnp.maximum(i * (OUT_TM // HALO) - 1, 0), col0 + off))
    in_specs = [row_spec(ATTN_WIDTH), z_tile(0), z_tile(1), z_tile(2), z_tile(3), z_halo(0), z_halo(2), z_halo(3),
                whole((len(POOL_WINDOWS), POOL_GROUP, POOL_GROUP)), whole((1, POOL_WIDTH)),
                whole((CONV_TAPS, CONV_WIDTH)),
                row_spec(D_MODEL), whole((D_MODEL, D_MODEL)), whole((1, D_MODEL))]
    out_specs = [row_spec(D_MODEL), row_spec(D_MODEL)]
    out_shape = [jax.ShapeDtypeStruct((n, D_MODEL), F32),
                 jax.ShapeDtypeStruct((n, D_MODEL), F32 if with_router else BF16)]
    args = [attn] + [z_rest] * 7 + [w_pool, pool_scale, conv_w, x, w_out, ffn_g]
    scratch = [pltpu.VMEM((OUT_TM, POOL_WIDTH + CONV_WIDTH), BF16)]
    if with_router:
        in_specs.append(whole((D_MODEL, LANES)))
        out_specs += [row_spec(LANES), whole((8, LANES))]
        out_shape += [jax.ShapeDtypeStruct((n, LANES), F32), jax.ShapeDtypeStruct((8, LANES), F32)]
        args.append(router_w)
        scratch += [pltpu.VMEM((1, LANES), F32), pltpu.VMEM((D_MODEL, 2 * LANES), BF16)]
    return pl.pallas_call(
        functools.partial(_out_proj_kernel, with_router=with_router, tiles_per_seq=seq // OUT_TM),
        grid=(n // OUT_TM,),
        in_specs=in_specs,
        out_specs=out_specs,
        out_shape=out_shape,
        scratch_shapes=scratch,
        compiler_params=_params(("arbitrary",) if with_router else ("parallel",)),
        name="out_proj_router" if with_router else "out_proj",
    )(*args)


def _wait_rows(src_rows, dst_rows, sem):
    pltpu.make_async_copy(src_rows, dst_rows, sem).wait()


SUBLANES = 8
PAD_CHUNK_BITS = (FFN_TM // SUBLANES // 2).bit_length()


def _pad_fill_copies(pad_start_ref, pad_len_ref, nt_ref, zero_ref, xs_ref, sem, act):
    half = zero_ref.shape[0]

    def dead_tile(t, carry):
        for part in range(FFN_TM // half):
            rows = pl.ds(pl.multiple_of(t * FFN_TM + part * half, half), half)
            act(pltpu.make_async_copy(zero_ref, xs_ref.at[rows], sem))
        return carry

    lax.fori_loop(nt_ref[0], xs_ref.shape[0] // FFN_TM, dead_tile, 0)
    for e in range(N_EXPERTS):
        start, length = pad_start_ref[e], pad_len_ref[e]
        head = jnp.minimum((SUBLANES - start % SUBLANES) % SUBLANES, length)
        for i in range(SUBLANES - 1):
            @pl.when(i < head)
            def _():
                act(pltpu.make_async_copy(zero_ref.at[pl.ds(0, 1)], xs_ref.at[pl.ds(start + i, 1)], sem))

        aligned = start + head
        groups = (length - head) // SUBLANES
        for b in reversed(range(PAD_CHUNK_BITS)):
            size = SUBLANES << b

            @pl.when(((groups >> b) & 1) == 1)
            def _():
                covered = ((groups >> (b + 1)) << (b + 1)) * SUBLANES
                rows = pl.ds(pl.multiple_of(aligned + covered, SUBLANES), size)
                act(pltpu.make_async_copy(zero_ref.at[pl.ds(0, size)], xs_ref.at[rows], sem))


def _dispatch_kernel(slot_ref, pad_start_ref, pad_len_ref, nt_ref, h_ref, xs_ref, zero_ref, sem, pad_sem):
    rows = h_ref.shape[0]
    base = pl.program_id(0) * rows * TOP_K
    pad_args = (pad_start_ref, pad_len_ref, nt_ref, zero_ref, xs_ref, pad_sem)

    @pl.when(pl.program_id(0) == 0)
    def _():
        zero_ref[...] = jnp.zeros(zero_ref.shape, F32)
        _pad_fill_copies(*pad_args, lambda copy: copy.start())

    def issue(r, carry):
        for k in range(TOP_K):
            s = slot_ref[base + r * TOP_K + k]
            pltpu.make_async_copy(h_ref.at[pl.ds(r, 1)], xs_ref.at[pl.ds(s, 1)], sem).start()
        return carry

    lax.fori_loop(0, rows, issue, 0, unroll=8)
    for _ in range(TOP_K):
        _wait_rows(h_ref, xs_ref.at[pl.ds(0, rows)], sem)

    @pl.when(pl.program_id(0) == 0)
    def _():
        _pad_fill_copies(*pad_args, lambda copy: copy.wait())


def _dispatch(slot, pad_start, pad_len, n_tiles, h, n_slots):
    n = h.shape[0]
    grid_spec = pltpu.PrefetchScalarGridSpec(
        num_scalar_prefetch=4,
        grid=(n // PERM_TM,),
        in_specs=[pl.BlockSpec((PERM_TM, D_MODEL), lambda i, s, ps, pn, nt: (i, 0))],
        out_specs=pl.BlockSpec(memory_space=pl.ANY),
        scratch_shapes=[pltpu.VMEM((FFN_TM // 2, D_MODEL), F32), pltpu.SemaphoreType.DMA(()),
                        pltpu.SemaphoreType.DMA(())],
    )
    return pl.pallas_call(
        _dispatch_kernel,
        grid_spec=grid_spec,
        out_shape=jax.ShapeDtypeStruct((n_slots, D_MODEL), F32),
        compiler_params=_params(("arbitrary",)),
        name="moe_dispatch",
    )(slot, pad_start, pad_len, n_tiles, h)


def _combine_kernel(slot_ref, x_ref, route_ref, y_ref, o_ref, rows_ref, sems):
    rows = x_ref.shape[0]
    step = pl.program_id(0)

    def start_gather(of_step):
        buf = of_step % 2
        base = of_step * rows * TOP_K

        def issue(r, carry):
            for k in range(TOP_K):
                s = slot_ref[base + r * TOP_K + k]
                pltpu.make_async_copy(y_ref.at[pl.ds(s, 1)], rows_ref.at[buf, k, pl.ds(r, 1)], sems.at[buf]).start()
            return carry

        lax.fori_loop(0, rows, issue, 0, unroll=8)

    @pl.when(step == 0)
    def _():
        start_gather(step)

    @pl.when(step + 1 < pl.num_programs(0))
    def _():
        start_gather(step + 1)

    buf = step % 2
    for k in range(TOP_K):
        _wait_rows(y_ref.at[pl.ds(0, rows)], rows_ref.at[buf, k], sems.at[buf])
    route = route_ref[...]
    out = x_ref[...]
    for k in range(TOP_K):
        out = out + route[:, ROUTE_GATE + k:ROUTE_GATE + k + 1] * rows_ref[buf, k]
    o_ref[...] = out


def _combine(slot, x, route, y):
    n = x.shape[0]
    grid_spec = pltpu.PrefetchScalarGridSpec(
        num_scalar_prefetch=1,
        grid=(n // PERM_TM,),
        in_specs=[
            pl.BlockSpec((PERM_TM, D_MODEL), lambda i, s: (i, 0)),
            pl.BlockSpec((PERM_TM, LANES), lambda i, s: (i, 0)),
            pl.BlockSpec(memory_space=pl.ANY),
        ],
        out_specs=pl.BlockSpec((PERM_TM, D_MODEL), lambda i, s: (i, 0)),
        scratch_shapes=[pltpu.VMEM((2, TOP_K, PERM_TM, D_MODEL), F32), pltpu.SemaphoreType.DMA((2,))],
    )
    return pl.pallas_call(
        _combine_kernel,
        grid_spec=grid_spec,
        out_shape=jax.ShapeDtypeStruct((n, D_MODEL), F32),
        compiler_params=_params(("arbitrary",)),
        name="moe_combine",
    )(slot, x, route, y)


def _swiglu_rows(x, wg_ref, wu_ref, wd_ref):
    g = jnp.dot(x, wg_ref[0].astype(BF16), preferred_element_type=F32)
    u = jnp.dot(x, wu_ref[0].astype(BF16), preferred_element_type=F32)
    a = (g * jax.nn.sigmoid(g) * u).astype(BF16)
    return jnp.dot(a, wd_ref[0].astype(BF16), preferred_element_type=F32)


def _token_tile(ref, t):
    return ref.at[pl.ds(pl.multiple_of(t * FFN_TM, FFN_TM), FFN_TM)]


def _dense_ffn_kernel(x_ref, wg_ref, wu_ref, wd_ref, res_ref, o_ref, res_buf, sem, *, n_f):
    t = pl.program_id(0)
    f = pl.program_id(1)
    res_copy = pltpu.make_async_copy(_token_tile(res_ref, t), res_buf, sem)

    @pl.when(f == 0)
    def _():
        res_copy.start()
        o_ref[...] = _swiglu_rows(x_ref[...], wg_ref, wu_ref, wd_ref)

    @pl.when(f > 0)
    def _():
        o_ref[...] += _swiglu_rows(x_ref[...], wg_ref, wu_ref, wd_ref)

    @pl.when(f == n_f - 1)
    def _():
        res_copy.wait()
        o_ref[...] += res_buf[...]


def _dense_ffn(h, wg, wu, wd, res):
    rows = h.shape[0]
    n_f = wg.shape[-1] // FFN_TF
    return pl.pallas_call(
        functools.partial(_dense_ffn_kernel, n_f=n_f),
        grid=(rows // FFN_TM, n_f),
        in_specs=[
            pl.BlockSpec((FFN_TM, D_MODEL), lambda t, f: (t, 0)),
            pl.BlockSpec((1, D_MODEL, FFN_TF), lambda t, f: (0, 0, f)),
            pl.BlockSpec((1, D_MODEL, FFN_TF), lambda t, f: (0, 0, f)),
            pl.BlockSpec((1, FFN_TF, D_MODEL), lambda t, f: (0, f, 0)),
            pl.BlockSpec(memory_space=pl.ANY),
        ],
        out_specs=pl.BlockSpec((FFN_TM, D_MODEL), lambda t, f: (t, 0)),
        out_shape=jax.ShapeDtypeStruct((rows, D_MODEL), F32),
        scratch_shapes=[pltpu.VMEM((FFN_TM, D_MODEL), F32), pltpu.SemaphoreType.DMA(())],
        compiler_params=_params(("arbitrary", "arbitrary")),
        name="dense_ffn",
    )(h, wg, wu, wd, res)


def _moe_ffn_kernel(te_ref, nt_ref, tv_ref, x_ref, wg_ref, wu_ref, wd_ref, o_ref, stage_ref, xb_ref, sem):
    t = pl.program_id(0)
    f = pl.program_id(1)
    live = t < nt_ref[0]
    valid = tv_ref[t]

    def x_copy(tile):
        return pltpu.make_async_copy(_token_tile(x_ref, tile), stage_ref, sem)

    @pl.when(f == 0)
    def _():
        o_ref[...] = jnp.zeros(o_ref.shape, F32)

        @pl.when(t == 0)
        def _():
            x_copy(t).start()

        @pl.when(live)
        def _():
            x_copy(t).wait()
            xb_ref[...] = stage_ref[...].astype(BF16)

    @pl.when((f == 1) & (t + 1 < nt_ref[0]))
    def _():
        x_copy(t + 1).start()

    for rows, applies in ((slice(0, FFN_TM), valid > FFN_TM // 2), (slice(0, FFN_TM // 2), valid <= FFN_TM // 2)):
        @pl.when(live & applies)
        def _():
            o_ref[rows, :] += _swiglu_rows(xb_ref[rows, :], wg_ref, wu_ref, wd_ref)


def _moe_ffn(tile_expert, n_tiles, tile_valid, x_sorted, wg, wu, wd):
    rows = x_sorted.shape[0]
    n_f = wg.shape[-1] // MOE_TF

    def hidden(t, f, nt):
        return jnp.where(t < nt[0], f, n_f - 1)

    grid_spec = pltpu.PrefetchScalarGridSpec(
        num_scalar_prefetch=3,
        grid=(rows // FFN_TM, n_f),
        in_specs=[
            pl.BlockSpec(memory_space=pl.ANY),
            pl.BlockSpec((1, D_MODEL, MOE_TF), lambda t, f, te, nt, tv: (te[t], 0, hidden(t, f, nt))),
            pl.BlockSpec((1, D_MODEL, MOE_TF), lambda t, f, te, nt, tv: (te[t], 0, hidden(t, f, nt))),
            pl.BlockSpec((1, MOE_TF, D_MODEL), lambda t, f, te, nt, tv: (te[t], hidden(t, f, nt), 0)),
        ],
        out_specs=pl.BlockSpec((FFN_TM, D_MODEL), lambda t, f, te, nt, tv: (t, 0)),
        scratch_shapes=[pltpu.VMEM((FFN_TM, D_MODEL), F32), pltpu.VMEM((FFN_TM, D_MODEL), BF16),
                        pltpu.SemaphoreType.DMA(())],
    )
    return pl.pallas_call(
        _moe_ffn_kernel,
        grid_spec=grid_spec,
        out_shape=jax.ShapeDtypeStruct((rows, D_MODEL), F32),
        compiler_params=_params(("arbitrary", "arbitrary"), MOE_VMEM_LIMIT_BYTES),
        name="moe_ffn",
    )(tile_expert, n_tiles, tile_valid, x_sorted, wg, wu, wd)


def _moe_layer(x, h, route, counts, wg, wu, wd):
    n = x.shape[0]
    n_t = (n * TOP_K) // FFN_TM + N_EXPERTS
    experts = jnp.arange(N_EXPERTS, dtype=jnp.int32)
    counts = counts[0, :N_EXPERTS].astype(jnp.int32)
    tiles_per = (counts + FFN_TM - 1) // FFN_TM
    tiles_end = jnp.cumsum(tiles_per)
    first_tile = tiles_end - tiles_per
    idx = route[:, ROUTE_EXPERT:ROUTE_EXPERT + TOP_K].astype(jnp.int32)
    rank = route[:, ROUTE_RANK:ROUTE_RANK + TOP_K].astype(jnp.int32)
    group_start = jnp.sum(jnp.where(idx[:, :, None] == experts, first_tile * FFN_TM, 0), axis=-1)
    slot = (group_start + rank).reshape(-1)

    n_tiles = tiles_end[-1:]
    t_ids = jnp.minimum(jnp.arange(n_t, dtype=jnp.int32), n_tiles[0] - 1)
    tile_expert = jnp.minimum(jnp.sum((t_ids[:, None] >= tiles_end[None, :]).astype(jnp.int32), axis=1),
                              N_EXPERTS - 1)
    of_tile = tile_expert[:, None] == experts[None, :]
    tile_valid = jnp.clip(jnp.sum(jnp.where(of_tile, counts - (t_ids[:, None] - first_tile) * FFN_TM, 0), axis=1),
                          0, FFN_TM)
    pad_start = first_tile * FFN_TM + counts
    pad_len = tiles_per * FFN_TM - counts

    x_sorted = _dispatch(slot, pad_start, pad_len, n_tiles, h, n_t * FFN_TM)
    y = _moe_ffn(tile_expert, n_tiles, tile_valid, x_sorted, wg, wu, wd)
    return _combine(slot, x, route, y)


def kernel(x, positions, attn_norm_g, w_in, q_norm_g, k_norm_g, lambda_vecs, attn_out_norm_g, w_pool, pool_scale,
           conv_w, w_out, ffn_norm_g, dense_w_gate, dense_w_up, dense_w_down, router_w, moe_w_gate, moe_w_up,
           moe_w_down):
    batch, seq, d_model = x.shape
    depth = w_in.shape[0]
    n = batch * seq

    inv_freq = 1.0 / (ROPE_THETA ** (jnp.arange(0, DIFF_HEAD_DIM, 2, dtype=F32) / DIFF_HEAD_DIM))
    ang = positions.astype(F32)[..., None] * inv_freq
    cos = jnp.cos(ang).astype(x.dtype).reshape(n, DIFF_HEAD_DIM // 2)
    sin = jnp.sin(ang).astype(x.dtype).reshape(n, DIFF_HEAD_DIM // 2)
    cos2 = jnp.concatenate([cos, cos], axis=-1)
    sin2 = jnp.concatenate([-sin, sin], axis=-1)

    xf = x.reshape(n, d_model)
    for l in range(depth):
        lam_init = 0.8 - 0.6 * math.exp(-0.3 * l)
        qk_g = jnp.stack([q_norm_g[l], k_norm_g[l]])[:, None, :]
        z_qk, z_rest = _in_proj(xf, attn_norm_g[l][None, :], w_in[l].astype(BF16), cos2, sin2, qk_g)
        attn = _attention(z_qk, z_rest, lambda_vecs[l], attn_out_norm_g[l][None, :], lam_init, batch, seq)
        out_proj_args = (attn, z_rest, w_pool[l].astype(BF16), pool_scale[l][None, :], conv_w[l], xf,
                         w_out[l].astype(BF16), ffn_norm_g[l][None, :], seq)
        j = l // 2
        if l % 2 == 0:
            xf, h = _out_proj(*out_proj_args)
            xf = _dense_ffn(h, dense_w_gate[j][None], dense_w_up[j][None], dense_w_down[j][None], xf)
        else:
            rw = jnp.pad(router_w[j], ((0, 0), (0, LANES - N_EXPERTS)))
            xf, h, route, counts = _out_proj(*out_proj_args, router_w=rw)
            xf = _moe_layer(xf, h, route, counts, moe_w_gate[j], moe_w_up[j], moe_w_down[j])
    return xf.reshape(batch, seq, d_model)
```

```python
import functools
import math

import jax
import jax.numpy as jnp
from jax import lax
from jax.experimental import pallas as pl
from jax.experimental.pallas import tpu as pltpu

D_MODEL = 2048
DIFF_HEAD_DIM = 128
DIFF_HEADS = 4
HEAD_WIDTH = 2 * DIFF_HEAD_DIM
QK_WIDTH = DIFF_HEADS * HEAD_WIDTH
ATTN_WIDTH = QK_WIDTH
POOL_WINDOWS = (2, 4, 8, 16)
POOL_GROUP = 128
POOL_WIDTH = 512
CONV_WIDTH = 512
CONV_TAPS = 3
IN_WIDTH = 5120
ROPE_THETA = 10000.0
N_EXPERTS = 8
TOP_K = 2
EPS = 1e-6
NEG_INF = -1e30
LANES = 128
Q_SCALE = DIFF_HEAD_DIM ** -0.5 * math.log2(math.e)

BF16 = jnp.bfloat16
F32 = jnp.float32

VMEM_LIMIT_BYTES = 56 * 1024 * 1024
FFN_VMEM_LIMIT_BYTES = 60 * 1024 * 1024

IN_TM = 1024
IN_TN = 1024
ATTN_TQ = 512
OUT_TM = 512
FFN_TM = 1024
FFN_TF = 512
PERM_TM = 512


def _params(semantics, vmem_limit_bytes=VMEM_LIMIT_BYTES):
    return pltpu.CompilerParams(dimension_semantics=semantics, vmem_limit_bytes=vmem_limit_bytes)


def _in_proj_kernel(x_ref, g_ref, w_ref, cos_ref, sin_ref, qkg_ref, zqk_ref, zrest_ref, h_ref, rawq_ref, rawk_ref):
    j = pl.program_id(1)

    def project():
        return jnp.dot(h_ref[...], w_ref[...], preferred_element_type=F32)

    def head_norm_rotary(raw_ref, which):
        gain = qkg_ref[which]
        cos = cos_ref[...]
        sin = sin_ref[...]
        for c in range(IN_TN // DIFF_HEAD_DIM):
            cols = slice(c * DIFF_HEAD_DIM, (c + 1) * DIFF_HEAD_DIM)
            zc = raw_ref[:, cols]
            ms = jnp.mean(zc * zc, axis=-1, keepdims=True)
            zn = zc * lax.rsqrt(ms + EPS) * gain
            rot = zn * cos + pltpu.roll(zn, DIFF_HEAD_DIM // 2, axis=1) * sin
            if which == 0:
                rot = rot * Q_SCALE
            zqk_ref[:, cols] = rot.astype(BF16)

    @pl.when(j == 0)
    def _():
        x = x_ref[...]
        ms = jnp.mean(x * x, axis=-1, keepdims=True)
        h_ref[...] = (x * lax.rsqrt(ms + EPS) * g_ref[...]).astype(BF16)
        rawq_ref[...] = project()

    @pl.when(j == 1)
    def _():
        rawk_ref[...] = project()
        head_norm_rotary(rawq_ref, 0)

    @pl.when(j == 2)
    def _():
        zrest_ref[...] = project().astype(BF16)
        head_norm_rotary(rawk_ref, 1)

    @pl.when(j > 2)
    def _():
        zrest_ref[...] = project().astype(BF16)


def _in_proj(x, norm_g, w_in, cos2, sin2, qk_g):
    n = x.shape[0]
    n_qk = 2 * QK_WIDTH // IN_TN
    grid = (n // IN_TM, IN_WIDTH // IN_TN)
    return pl.pallas_call(
        _in_proj_kernel,
        grid=grid,
        in_specs=[
            pl.BlockSpec((IN_TM, D_MODEL), lambda i, j: (i, 0)),
            pl.BlockSpec((1, D_MODEL), lambda i, j: (0, 0)),
            pl.BlockSpec((D_MODEL, IN_TN), lambda i, j: (0, j)),
            pl.BlockSpec((IN_TM, DIFF_HEAD_DIM), lambda i, j: (i, 0)),
            pl.BlockSpec((IN_TM, DIFF_HEAD_DIM), lambda i, j: (i, 0)),
            pl.BlockSpec((2, 1, DIFF_HEAD_DIM), lambda i, j: (0, 0, 0)),
        ],
        out_specs=[
            pl.BlockSpec((IN_TM, IN_TN), lambda i, j: (i, jnp.clip(j - 1, 0, n_qk - 1))),
            pl.BlockSpec((IN_TM, IN_TN), lambda i, j: (i, jnp.maximum(j - n_qk, 0))),
        ],
        out_shape=[jax.ShapeDtypeStruct((n, 2 * QK_WIDTH), BF16),
                   jax.ShapeDtypeStruct((n, IN_WIDTH - 2 * QK_WIDTH), BF16)],
        scratch_shapes=[pltpu.VMEM((IN_TM, D_MODEL), BF16), pltpu.VMEM((IN_TM, IN_TN), F32),
                        pltpu.VMEM((IN_TM, IN_TN), F32)],
        compiler_params=_params(("parallel", "arbitrary")),
        name="in_proj",
    )(x, norm_g, w_in, cos2, sin2, qk_g)


def _attn_kernel(q_ref, k_ref, v_ref, lv_ref, subg_ref, o_ref, *, lam_init):
    t, d = ATTN_TQ, DIFF_HEAD_DIM
    seq = q_ref.shape[0]
    lv = lv_ref[...]
    lam = (jnp.exp(jnp.sum(lv[0:1] * lv[1:2], axis=-1, keepdims=True))
           - jnp.exp(jnp.sum(lv[2:3] * lv[3:4], axis=-1, keepdims=True)) + lam_init)
    causal = (lax.broadcasted_iota(jnp.int32, (t, t), 1) <= lax.broadcasted_iota(jnp.int32, (t, t), 0))
    nt_dims = (((1,), (1,)), ((), ()))

    for i in range(seq // t):
        lo = i * t
        a_diag, a_past = None, None
        for mi in range(2):
            cols = slice(mi * d, (mi + 1) * d)
            qm = q_ref[lo:lo + t, cols]
            s_diag = lax.dot_general(qm, k_ref[lo:lo + t, cols], nt_dims, preferred_element_type=F32)
            s_diag = jnp.where(causal, s_diag, NEG_INF)
            mx = jnp.max(s_diag, axis=-1, keepdims=True)
            if i > 0:
                s_past = lax.dot_general(qm, k_ref[0:lo, cols], nt_dims, preferred_element_type=F32)
                mx = jnp.maximum(mx, jnp.max(s_past, axis=-1, keepdims=True))
            p_diag = jnp.exp2(s_diag - mx)
            denom = jnp.sum(p_diag, axis=-1, keepdims=True)
            if i > 0:
                p_past = jnp.exp2(s_past - mx)
                denom = denom + jnp.sum(p_past, axis=-1, keepdims=True)
            weight = 1.0 / denom if mi == 0 else lam / denom
            if mi == 0:
                a_diag = p_diag * weight
                a_past = p_past * weight if i > 0 else None
            else:
                a_diag = a_diag - p_diag * weight
                a_past = a_past - p_past * weight if i > 0 else None
        o = jnp.dot(a_diag.astype(BF16), v_ref[lo:lo + t, :], preferred_element_type=F32)
        if i > 0:
            o = o + jnp.dot(a_past.astype(BF16), v_ref[0:lo, :], preferred_element_type=F32)
        ms = jnp.mean(o * o, axis=-1, keepdims=True)
        o_ref[lo:lo + t, :] = ((o * lax.rsqrt(ms + EPS)) * subg_ref[...] * (1.0 - lam_init)).astype(BF16)


def _attention(z_qk, z_rest, lam_vecs, sub_g, lam_init, batch, seq):
    k_col0 = QK_WIDTH // HEAD_WIDTH
    return pl.pallas_call(
        functools.partial(_attn_kernel, lam_init=lam_init),
        grid=(batch, DIFF_HEADS),
        in_specs=[
            pl.BlockSpec((seq, HEAD_WIDTH), lambda b, h: (b, h)),
            pl.BlockSpec((seq, HEAD_WIDTH), lambda b, h: (b, k_col0 + h)),
            pl.BlockSpec((seq, HEAD_WIDTH), lambda b, h: (b, h)),
            pl.BlockSpec((4, DIFF_HEAD_DIM), lambda b, h: (0, 0)),
            pl.BlockSpec((1, HEAD_WIDTH), lambda b, h: (0, 0)),
        ],
        out_specs=pl.BlockSpec((seq, HEAD_WIDTH), lambda b, h: (b, h)),
        out_shape=jax.ShapeDtypeStruct((batch * seq, ATTN_WIDTH), BF16),
        compiler_params=_params(("parallel", "parallel")),
        name="diff_attention",
    )(z_qk, z_qk, z_rest, lam_vecs, sub_g)


HALO = 16


def _mix_tile(pos0, up_ref, gb_ref, gc_ref, uc_ref, up_halo_ref, gc_halo_ref, uc_halo_ref, wp_ref, ps_ref, cw_ref,
              mix_ref):
    rows = up_ref.shape[0]
    pos = pos0 - HALO + lax.broadcasted_iota(jnp.int32, (HALO + rows, LANES), 0)

    def extended(halo_ref, ref, cols):
        x = jnp.concatenate([halo_ref[:, cols], ref[:, cols]], axis=0).astype(F32)
        return jnp.where(pos >= 0, x, 0.0)

    def delayed(x, k):
        return pltpu.roll(x, k, axis=0)

    for gi, w in enumerate(POOL_WINDOWS):
        cols = slice(gi * POOL_GROUP, (gi + 1) * POOL_GROUP)
        g = extended(up_halo_ref, up_ref, cols)
        s, span = g, 1
        while span < w:
            s = s + delayed(s, span)
            span *= 2
        count = jnp.clip(pos + 1, 1, w).astype(F32)
        pooled = (s / count - g)[HALO:]
        y = jnp.dot(pooled.astype(BF16), wp_ref[gi], preferred_element_type=F32) * ps_ref[:, cols]
        mix_ref[:, cols] = y.astype(BF16)

    for c in range(CONV_WIDTH // LANES):
        cols = slice(c * LANES, (c + 1) * LANES)
        u = extended(gc_halo_ref, gc_ref, cols) * extended(uc_halo_ref, uc_ref, cols)
        y = (cw_ref[0:1, cols] * delayed(u, 2) + cw_ref[1:2, cols] * delayed(u, 1)) + cw_ref[2:3, cols] * u
        out = gb_ref[:, cols].astype(F32) * y[HALO:]
        mix_ref[:, POOL_WIDTH + c * LANES:POOL_WIDTH + (c + 1) * LANES] = out.astype(BF16)


ROUTE_GATE, ROUTE_EXPERT, ROUTE_RANK = 0, 2, 4


N_MIX_REFS = 10


def _out_proj_kernel(*refs, with_router, tiles_per_seq):
    attn_ref, mix_in = refs[0], refs[1:1 + N_MIX_REFS]
    x_ref, wo_ref, g_ref, *rest = refs[1 + N_MIX_REFS:]
    if with_router:
        rw_ref, xo_ref, h_ref, route_ref, counts_ref, mix_ref, run_ref, r2_ref = rest
    else:
        xo_ref, h_ref, mix_ref = rest
    y = jnp.dot(attn_ref[...], wo_ref[:ATTN_WIDTH, :], preferred_element_type=F32)
    _mix_tile((pl.program_id(0) % tiles_per_seq) * OUT_TM, *mix_in, mix_ref)
    y = y + jnp.dot(mix_ref[...], wo_ref[ATTN_WIDTH:, :], preferred_element_type=F32)
    xn = x_ref[...] + y
    xo_ref[...] = xn
    ms = jnp.mean(xn * xn, axis=-1, keepdims=True)
    hf = xn * lax.rsqrt(ms + EPS) * g_ref[...]
    h_ref[...] = hf.astype(h_ref.dtype)
    if with_router:
        @pl.when(pl.program_id(0) == 0)
        def _():
            rw = rw_ref[...]
            r_hi = rw.astype(BF16)
            r2_ref[:, :LANES] = r_hi
            r2_ref[:, LANES:] = (rw - r_hi.astype(F32)).astype(BF16)
            run_ref[...] = jnp.zeros(run_ref.shape, F32)

        h_hi = hf.astype(BF16)
        h_lo = (hf - h_hi.astype(F32)).astype(BF16)
        hi_terms = jnp.dot(h_hi, r2_ref[...], preferred_element_type=F32)
        logits = (hi_terms[:, :LANES] + hi_terms[:, LANES:]
                  + jnp.dot(h_lo, r2_ref[:, :LANES], preferred_element_type=F32))
        tm = logits.shape[0]
        lane = lax.broadcasted_iota(jnp.int32, logits.shape, 1)
        logits = jnp.where(lane < N_EXPERTS, logits, -jnp.inf)
        v1 = jnp.max(logits, axis=-1, keepdims=True)
        i1 = jnp.min(jnp.where(logits == v1, lane, LANES), axis=-1, keepdims=True)
        rest = jnp.where(lane == i1, -jnp.inf, logits)
        v2 = jnp.max(rest, axis=-1, keepdims=True)
        i2 = jnp.min(jnp.where(rest == v2, lane, LANES), axis=-1, keepdims=True)
        e2 = jnp.exp(v2 - v1)
        g1 = 1.0 / (1.0 + e2)
        g2 = e2 / (1.0 + e2)

        sel1 = lane == i1
        sel2 = lane == i2
        chosen = jnp.where(sel1 | sel2, 1.0, 0.0)
        earlier = (lax.broadcasted_iota(jnp.int32, (tm, tm), 1) < lax.broadcasted_iota(jnp.int32, (tm, tm), 0))
        before = jnp.dot(earlier.astype(BF16), chosen.astype(BF16), preferred_element_type=F32) + run_ref[...]
        r1 = jnp.sum(jnp.where(sel1, before, 0.0), axis=-1, keepdims=True)
        r2 = jnp.sum(jnp.where(sel2, before, 0.0), axis=-1, keepdims=True)
        run_ref[...] += jnp.sum(chosen, axis=0, keepdims=True)
        counts_ref[...] = jnp.broadcast_to(run_ref[...], counts_ref.shape)

        fields = (g1, g2, i1.astype(F32), i2.astype(F32), r1, r2)
        route = jnp.zeros(logits.shape, F32)
        for k, val in enumerate(fields):
            route = jnp.where(lane == k, val, route)
        route_ref[...] = route


def _out_proj(attn, z_rest, w_pool, pool_scale, conv_w, x, w_out, ffn_g, seq, router_w=None):
    n = x.shape[0]
    with_router = router_w is not None
    row_spec = lambda width: pl.BlockSpec((OUT_TM, width), lambda i: (i, 0))
    whole = lambda shape: pl.BlockSpec(shape, lambda i: (0,) * len(shape))
    col0 = ATTN_WIDTH // POOL_WIDTH
    z_tile = lambda off: pl.BlockSpec((OUT_TM, POOL_WIDTH), lambda i: (i, col0 + off))
    z_halo = lambda off: pl.BlockSpec((HALO, POOL_WIDTH),
                                      lambda i: (jnp.maximum(i * (OUT_TM // HALO) - 1, 0), col0 + off))
    in_specs = [row_spec(ATTN_WIDTH), z_tile(0), z_tile(1), z_tile(2), z_tile(3), z_halo(0), z_halo(2), z_halo(3),
                whole((len(POOL_WINDOWS), POOL_GROUP, POOL_GROUP)), whole((1, POOL_WIDTH)),
                whole((CONV_TAPS, CONV_WIDTH)),
                row_spec(D_MODEL), whole((D_MODEL, D_MODEL)), whole((1, D_MODEL))]
    out_specs = [row_spec(D_MODEL), row_spec(D_MODEL)]
    out_shape = [jax.ShapeDtypeStruct((n, D_MODEL), F32),
                 jax.ShapeDtypeStruct((n, D_MODEL), F32 if with_router else BF16)]
    args = [attn] + [z_rest] * 7 + [w_pool, pool_scale, conv_w, x, w_out, ffn_g]
    scratch = [pltpu.VMEM((OUT_TM, POOL_WIDTH + CONV_WIDTH), BF16)]
    if with_router:
        in_specs.append(whole((D_MODEL, LANES)))
        out_specs += [row_spec(LANES), whole((8, LANES))]
        out_shape += [jax.ShapeDtypeStruct((n, LANES), F32), jax.ShapeDtypeStruct((8, LANES), F32)]
        args.append(router_w)
        scratch += [pltpu.VMEM((1, LANES), F32), pltpu.VMEM((D_MODEL, 2 * LANES), BF16)]
    return pl.pallas_call(
        functools.partial(_out_proj_kernel, with_router=with_router, tiles_per_seq=seq // OUT_TM),
        grid=(n // OUT_TM,),
        in_specs=in_specs,
        out_specs=out_specs,
        out_shape=out_shape,
        scratch_shapes=scratch,
        compiler_params=_params(("arbitrary",) if with_router else ("parallel",)),
        name="out_proj_router" if with_router else "out_proj",
    )(*args)


def _wait_rows(src_rows, dst_rows, sem):
    pltpu.make_async_copy(src_rows, dst_rows, sem).wait()


SUBLANES = 8
PAD_CHUNK_BITS = (FFN_TM // SUBLANES // 2).bit_length()


def _pad_fill_copies(pad_start_ref, pad_len_ref, nt_ref, zero_ref, xs_ref, sem, act):
    half = zero_ref.shape[0]

    def dead_tile(t, carry):
        for part in range(FFN_TM // half):
            rows = pl.ds(pl.multiple_of(t * FFN_TM + part * half, half), half)
            act(pltpu.make_async_copy(zero_ref, xs_ref.at[rows], sem))
        return carry

    lax.fori_loop(nt_ref[0], xs_ref.shape[0] // FFN_TM, dead_tile, 0)
    for e in range(N_EXPERTS):
        start, length = pad_start_ref[e], pad_len_ref[e]
        head = jnp.minimum((SUBLANES - start % SUBLANES) % SUBLANES, length)
        for i in range(SUBLANES - 1):
            @pl.when(i < head)
            def _():
                act(pltpu.make_async_copy(zero_ref.at[pl.ds(0, 1)], xs_ref.at[pl.ds(start + i, 1)], sem))

        aligned = start + head
        groups = (length - head) // SUBLANES
        for b in reversed(range(PAD_CHUNK_BITS)):
            size = SUBLANES << b

            @pl.when(((groups >> b) & 1) == 1)
            def _():
                covered = ((groups >> (b + 1)) << (b + 1)) * SUBLANES
                rows = pl.ds(pl.multiple_of(aligned + covered, SUBLANES), size)
                act(pltpu.make_async_copy(zero_ref.at[pl.ds(0, size)], xs_ref.at[rows], sem))


def _dispatch_kernel(slot_ref, pad_start_ref, pad_len_ref, nt_ref, h_ref, xs_ref, zero_ref, sem, pad_sem):
    rows = h_ref.shape[0]
    base = pl.program_id(0) * rows * TOP_K
    pad_args = (pad_start_ref, pad_len_ref, nt_ref, zero_ref, xs_ref, pad_sem)

    @pl.when(pl.program_id(0) == 0)
    def _():
        zero_ref[...] = jnp.zeros(zero_ref.shape, F32)
        _pad_fill_copies(*pad_args, lambda copy: copy.start())

    def issue(r, carry):
        for k in range(TOP_K):
            s = slot_ref[base + r * TOP_K + k]
            pltpu.make_async_copy(h_ref.at[pl.ds(r, 1)], xs_ref.at[pl.ds(s, 1)], sem).start()
        return carry

    lax.fori_loop(0, rows, issue, 0, unroll=8)
    for _ in range(TOP_K):
        _wait_rows(h_ref, xs_ref.at[pl.ds(0, rows)], sem)

    @pl.when(pl.program_id(0) == 0)
    def _():
        _pad_fill_copies(*pad_args, lambda copy: copy.wait())


def _dispatch(slot, pad_start, pad_len, n_tiles, h, n_slots):
    n = h.shape[0]
    grid_spec = pltpu.PrefetchScalarGridSpec(
        num_scalar_prefetch=4,
        grid=(n // PERM_TM,),
        in_specs=[pl.BlockSpec((PERM_TM, D_MODEL), lambda i, s, ps, pn, nt: (i, 0))],
        out_specs=pl.BlockSpec(memory_space=pl.ANY),
        scratch_shapes=[pltpu.VMEM((FFN_TM // 2, D_MODEL), F32), pltpu.SemaphoreType.DMA(()),
                        pltpu.SemaphoreType.DMA(())],
    )
    return pl.pallas_call(
        _dispatch_kernel,
        grid_spec=grid_spec,
        out_shape=jax.ShapeDtypeStruct((n_slots, D_MODEL), F32),
        compiler_params=_params(("arbitrary",)),
        name="moe_dispatch",
    )(slot, pad_start, pad_len, n_tiles, h)


def _combine_kernel(slot_ref, x_ref, route_ref, y_ref, o_ref, rows_ref, sems):
    rows = x_ref.shape[0]
    step = pl.program_id(0)

    def start_gather(of_step):
        buf = of_step % 2
        base = of_step * rows * TOP_K

        def issue(r, carry):
            for k in range(TOP_K):
                s = slot_ref[base + r * TOP_K + k]
                pltpu.make_async_copy(y_ref.at[pl.ds(s, 1)], rows_ref.at[buf, k, pl.ds(r, 1)], sems.at[buf]).start()
            return carry

        lax.fori_loop(0, rows, issue, 0, unroll=8)

    @pl.when(step == 0)
    def _():
        start_gather(step)

    @pl.when(step + 1 < pl.num_programs(0))
    def _():
        start_gather(step + 1)

    buf = step % 2
    for k in range(TOP_K):
        _wait_rows(y_ref.at[pl.ds(0, rows)], rows_ref.at[buf, k], sems.at[buf])
    route = route_ref[...]
    out = x_ref[...]
    for k in range(TOP_K):
        out = out + route[:, ROUTE_GATE + k:ROUTE_GATE + k + 1] * rows_ref[buf, k]
    o_ref[...] = out


def _combine(slot, x, route, y):
    n = x.shape[0]
    grid_spec = pltpu.PrefetchScalarGridSpec(
        num_scalar_prefetch=1,
        grid=(n // PERM_TM,),
        in_specs=[
            pl.BlockSpec((PERM_TM, D_MODEL), lambda i, s: (i, 0)),
            pl.BlockSpec((PERM_TM, LANES), lambda i, s: (i, 0)),
            pl.BlockSpec(memory_space=pl.ANY),
        ],
        out_specs=pl.BlockSpec((PERM_TM, D_MODEL), lambda i, s: (i, 0)),
        scratch_shapes=[pltpu.VMEM((2, TOP_K, PERM_TM, D_MODEL), F32), pltpu.SemaphoreType.DMA((2,))],
    )
    return pl.pallas_call(
        _combine_kernel,
        grid_spec=grid_spec,
        out_shape=jax.ShapeDtypeStruct((n, D_MODEL), F32),
        compiler_params=_params(("arbitrary",)),
        name="moe_combine",
    )(slot, x, route, y)


def _swiglu_rows(x, wg_ref, wu_ref, wd_ref):
    g = jnp.dot(x, wg_ref[0].astype(BF16), preferred_element_type=F32)
    u = jnp.dot(x, wu_ref[0].astype(BF16), preferred_element_type=F32)
    a = (g * jax.nn.sigmoid(g) * u).astype(BF16)
    return jnp.dot(a, wd_ref[0].astype(BF16), preferred_element_type=F32)


def _token_tile(ref, t):
    return ref.at[pl.ds(pl.multiple_of(t * FFN_TM, FFN_TM), FFN_TM)]


def _dense_ffn_kernel(x_ref, wg_ref, wu_ref, wd_ref, res_ref, o_ref, sem):
    t = pl.program_id(0)
    f = pl.program_id(1)

    @pl.when(f == 0)
    def _():
        res_copy = pltpu.make_async_copy(_token_tile(res_ref, t), o_ref, sem)
        res_copy.start()
        update = _swiglu_rows(x_ref[...], wg_ref, wu_ref, wd_ref)
        res_copy.wait()
        o_ref[...] += update

    @pl.when(f > 0)
    def _():
        o_ref[...] += _swiglu_rows(x_ref[...], wg_ref, wu_ref, wd_ref)


def _dense_ffn(h, wg, wu, wd, res):
    rows = h.shape[0]
    n_f = wg.shape[-1] // FFN_TF
    return pl.pallas_call(
        _dense_ffn_kernel,
        grid=(rows // FFN_TM, n_f),
        in_specs=[
            pl.BlockSpec((FFN_TM, D_MODEL), lambda t, f: (t, 0)),
            pl.BlockSpec((1, D_MODEL, FFN_TF), lambda t, f: (0, 0, f)),
            pl.BlockSpec((1, D_MODEL, FFN_TF), lambda t, f: (0, 0, f)),
            pl.BlockSpec((1, FFN_TF, D_MODEL), lambda t, f: (0, f, 0)),
            pl.BlockSpec(memory_space=pl.ANY),
        ],
        out_specs=pl.BlockSpec((FFN_TM, D_MODEL), lambda t, f: (t, 0)),
        out_shape=jax.ShapeDtypeStruct((rows, D_MODEL), F32),
        scratch_shapes=[pltpu.SemaphoreType.DMA(())],
        compiler_params=_params(("arbitrary", "arbitrary"), FFN_VMEM_LIMIT_BYTES),
        name="dense_ffn",
    )(h, wg, wu, wd, res)


def _moe_ffn_kernel(te_ref, nt_ref, tv_ref, x_ref, wg_ref, wu_ref, wd_ref, o_ref, stage_ref, xb_ref, sem):
    t = pl.program_id(0)
    f = pl.program_id(1)
    live = t < nt_ref[0]
    valid = tv_ref[t]

    def x_copy(tile):
        return pltpu.make_async_copy(_token_tile(x_ref, tile), stage_ref, sem)

    @pl.when(f == 0)
    def _():
        o_ref[...] = jnp.zeros(o_ref.shape, F32)

        @pl.when(t == 0)
        def _():
            x_copy(t).start()

        @pl.when(live)
        def _():
            x_copy(t).wait()
            xb_ref[...] = stage_ref[...].astype(BF16)

    @pl.when((f == 1) & (t + 1 < nt_ref[0]))
    def _():
        x_copy(t + 1).start()

    for rows, applies in ((slice(0, FFN_TM), valid > FFN_TM // 2), (slice(0, FFN_TM // 2), valid <= FFN_TM // 2)):
        @pl.when(live & applies)
        def _():
            o_ref[rows, :] += _swiglu_rows(xb_ref[rows, :], wg_ref, wu_ref, wd_ref)


def _moe_ffn(tile_expert, n_tiles, tile_valid, x_sorted, wg, wu, wd):
    rows = x_sorted.shape[0]
    n_f = wg.shape[-1] // FFN_TF

    def hidden(t, f, nt):
        return jnp.where(t < nt[0], f, n_f - 1)

    grid_spec = pltpu.PrefetchScalarGridSpec(
        num_scalar_prefetch=3,
        grid=(rows // FFN_TM, n_f),
        in_specs=[
            pl.BlockSpec(memory_space=pl.ANY),
            pl.BlockSpec((1, D_MODEL, FFN_TF), lambda t, f, te, nt, tv: (te[t], 0, hidden(t, f, nt))),
            pl.BlockSpec((1, D_MODEL, FFN_TF), lambda t, f, te, nt, tv: (te[t], 0, hidden(t, f, nt))),
            pl.BlockSpec((1, FFN_TF, D_MODEL), lambda t, f, te, nt, tv: (te[t], hidden(t, f, nt), 0)),
        ],
        out_specs=pl.BlockSpec((FFN_TM, D_MODEL), lambda t, f, te, nt, tv: (t, 0)),
        scratch_shapes=[pltpu.VMEM((FFN_TM, D_MODEL), F32), pltpu.VMEM((FFN_TM, D_MODEL), BF16),
                        pltpu.SemaphoreType.DMA(())],
    )
    return pl.pallas_call(
        _moe_ffn_kernel,
        grid_spec=grid_spec,
        out_shape=jax.ShapeDtypeStruct((rows, D_MODEL), F32),
        compiler_params=_params(("arbitrary", "arbitrary"), FFN_VMEM_LIMIT_BYTES),
        name="moe_ffn",
    )(tile_expert, n_tiles, tile_valid, x_sorted, wg, wu, wd)


def _moe_layer(x, h, route, counts, wg, wu, wd):
    n = x.shape[0]
    n_t = (n * TOP_K) // FFN_TM + N_EXPERTS
    experts = jnp.arange(N_EXPERTS, dtype=jnp.int32)
    counts = counts[0, :N_EXPERTS].astype(jnp.int32)
    tiles_per = (counts + FFN_TM - 1) // FFN_TM
    tiles_end = jnp.cumsum(tiles_per)
    first_tile = tiles_end - tiles_per
    idx = route[:, ROUTE_EXPERT:ROUTE_EXPERT + TOP_K].astype(jnp.int32)
    rank = route[:, ROUTE_RANK:ROUTE_RANK + TOP_K].astype(jnp.int32)
    group_start = jnp.sum(jnp.where(idx[:, :, None] == experts, first_tile * FFN_TM, 0), axis=-1)
    slot = (group_start + rank).reshape(-1)

    n_tiles = tiles_end[-1:]
    t_ids = jnp.minimum(jnp.arange(n_t, dtype=jnp.int32), n_tiles[0] - 1)
    tile_expert = jnp.minimum(jnp.sum((t_ids[:, None] >= tiles_end[None, :]).astype(jnp.int32), axis=1),
                              N_EXPERTS - 1)
    of_tile = tile_expert[:, None] == experts[None, :]
    tile_valid = jnp.clip(jnp.sum(jnp.where(of_tile, counts - (t_ids[:, None] - first_tile) * FFN_TM, 0), axis=1),
                          0, FFN_TM)
    pad_start = first_tile * FFN_TM + counts
    pad_len = tiles_per * FFN_TM - counts

    x_sorted = _dispatch(slot, pad_start, pad_len, n_tiles, h, n_t * FFN_TM)
    y = _moe_ffn(tile_expert, n_tiles, tile_valid, x_sorted, wg, wu, wd)
    return _combine(slot, x, route, y)


def kernel(x, positions, attn_norm_g, w_in, q_norm_g, k_norm_g, lambda_vecs, attn_out_norm_g, w_pool, pool_scale,
           conv_w, w_out, ffn_norm_g, dense_w_gate, dense_w_up, dense_w_down, router_w, moe_w_gate, moe_w_up,
           moe_w_down):
    batch, seq, d_model = x.shape
    depth = w_in.shape[0]
    n = batch * seq

    inv_freq = 1.0 / (ROPE_THETA ** (jnp.arange(0, DIFF_HEAD_DIM, 2, dtype=F32) / DIFF_HEAD_DIM))
    ang = positions.astype(F32)[..., None] * inv_freq
    cos = jnp.cos(ang).astype(x.dtype).reshape(n, DIFF_HEAD_DIM // 2)
    sin = jnp.sin(ang).astype(x.dtype).reshape(n, DIFF_HEAD_DIM // 2)
    cos2 = jnp.concatenate([cos, cos], axis=-1)
    sin2 = jnp.concatenate([-sin, sin], axis=-1)

    xf = x.reshape(n, d_model)
    for l in range(depth):
        lam_init = 0.8 - 0.6 * math.exp(-0.3 * l)
        qk_g = jnp.stack([q_norm_g[l], k_norm_g[l]])[:, None, :]
        z_qk, z_rest = _in_proj(xf, attn_norm_g[l][None, :], w_in[l].astype(BF16), cos2, sin2, qk_g)
        attn = _attention(z_qk, z_rest, lambda_vecs[l], attn_out_norm_g[l][None, :], lam_init, batch, seq)
        out_proj_args = (attn, z_rest, w_pool[l].astype(BF16), pool_scale[l][None, :], conv_w[l], xf,
                         w_out[l].astype(BF16), ffn_norm_g[l][None, :], seq)
        j = l // 2
        if l % 2 == 0:
            xf, h = _out_proj(*out_proj_args)
            xf = _dense_ffn(h, dense_w_gate[j][None], dense_w_up[j][None], dense_w_down[j][None], xf)
        else:
            rw = jnp.pad(router_w[j], ((0, 0), (0, LANES - N_EXPERTS)))
            xf, h, route, counts = _out_proj(*out_proj_args, router_w=rw)
            xf = _moe_layer(xf, h, route, counts, moe_w_gate[j], moe_w_up[j], moe_w_down[j])
    return xf.reshape(batch, seq, d_model)
```
